```python
import jax, jax.numpy as jnp
from jax import lax
import numpy as np

D_MODEL = 4096
BATCH = 1
SEQ = 16384
DEPTH = 4

EPS = 1e-6
N_AB_LAYERS = (DEPTH + 1) // 2
N_MLA_LAYERS = DEPTH // 2

CONV_WIDTH = D_MODEL // 2
CONV_TAPS = 3
GLA_HEADS = 8
GLA_VDIM = D_MODEL // 2
GLA_KDIM = GLA_VDIM // 2
GLA_DK = GLA_KDIM // GLA_HEADS
GLA_DV = GLA_VDIM // GLA_HEADS
GLA_GATE_RANK = 16
GLA_GATE_TAU = 16.0
GLA_CHUNK = 64
AB_SPLIT_SIZES = (CONV_WIDTH, CONV_WIDTH, CONV_WIDTH, GLA_KDIM, GLA_KDIM, GLA_VDIM, GLA_VDIM, 2 * GLA_GATE_RANK)
AB_IN_DIM = int(sum(AB_SPLIT_SIZES))
AB_SPLIT_POINTS = tuple(int(s) for s in np.cumsum(AB_SPLIT_SIZES)[:-1])
AB_OUT_IN = CONV_WIDTH + GLA_VDIM
MLA_HEADS = D_MODEL // 128
MLA_Q_LORA = D_MODEL // 4
MLA_KV_LORA = D_MODEL // 8
MLA_NOPE = 128
MLA_ROPE = 64
MLA_V = 128
MLA_DOWN_DIM = MLA_Q_LORA + MLA_KV_LORA + MLA_ROPE
ROPE_THETA = 10000.0
Q_BLOCK = 128
N_EXPERTS = 16
CAPACITY_FACTOR = 2
EXPERT_FF = 3 * D_MODEL // 16
OUT_SCALE = (2 * DEPTH) ** -0.5

kernel_name = "hybrid_conv_gla_mla_ec_encoder"


def rms_norm(t, gain):
    tf = t.astype(jnp.float32)
    y = tf * lax.rsqrt(jnp.mean(tf * tf, axis=-1, keepdims=True) + EPS)
    return (y * gain.astype(jnp.float32)).astype(t.dtype)


def short_conv_centred(u, w):
    s = u.shape[1]
    up = jnp.pad(u, ((0, 0), (1, 1), (0, 0)))
    return w[0] * up[:, :s] + w[1] * up[:, 1:s + 1] + w[2] * up[:, 2:]


def gla_chunked(q, k, v, log_a, strict):
    f32 = jnp.float32
    b_, h_, s_, dk = q.shape
    dv = v.shape[-1]
    n = s_ // GLA_CHUNK
    L = GLA_CHUNK
    q = q.astype(f32).reshape(b_, h_, n, L, dk)
    k = k.astype(f32).reshape(b_, h_, n, L, dk)
    v = v.astype(f32).reshape(b_, h_, n, L, dv)
    cum = jnp.cumsum(log_a.astype(f32).reshape(b_, h_, n, L, dk), axis=3)
    cum_last = cum[:, :, :, -1:, :]
    q_dec = q * jnp.exp(cum)
    k_dec = k * jnp.exp(-cum)
    k_end = k * jnp.exp(cum_last - cum)
    decay = jnp.exp(cum_last[:, :, :, 0, :])
    mask = jnp.tril(jnp.ones((L, L), dtype=bool), k=-1 if strict else 0)
    scores = jnp.where(mask, jnp.einsum('bhnid,bhnjd->bhnij', q_dec, k_dec), 0.0)
    o_intra = jnp.einsum('bhnij,bhnje->bhnie', scores, v)

    def step(state, inp):
        q_n, k_n, v_n, dec_n = inp
        o_n = jnp.einsum('bhld,bhde->bhle', q_n, state)
        state = dec_n[..., None] * state + jnp.einsum('bhld,bhle->bhde', k_n, v_n)
        return state, o_n

    xs = (jnp.moveaxis(q_dec, 2, 0), jnp.moveaxis(k_end, 2, 0),
          jnp.moveaxis(v, 2, 0), jnp.moveaxis(decay, 2, 0))
    _, o_inter = lax.scan(step, jnp.zeros((b_, h_, dk, dv), f32), xs)
    o = o_intra + jnp.moveaxis(o_inter, 0, 2)
    return o.reshape(b_, h_, s_, dv)


def conv_gla_mixer(h, w_in, conv_w, wa2_f, ba_f, wa2_b, ba_b, head_norm, w_out):
    b_, s_, _ = h.shape
    proj = h @ w_in
    b_gate, c_gate, u, q, k, v, g, a_lr = jnp.split(proj, AB_SPLIT_POINTS, axis=-1)
    y_conv = b_gate * short_conv_centred(c_gate * u, conv_w)
    a_f, a_b = jnp.split(a_lr, 2, axis=-1)
    la_f = jax.nn.log_sigmoid((a_f @ wa2_f + ba_f).astype(jnp.float32)) / GLA_GATE_TAU
    la_b = jax.nn.log_sigmoid((a_b @ wa2_b + ba_b).astype(jnp.float32)) / GLA_GATE_TAU

    def heads(t, d):
        return t.reshape(b_, s_, GLA_HEADS, d).transpose(0, 2, 1, 3)

    qh = heads(q, GLA_DK) * (GLA_DK ** -0.5)
    kh = heads(k, GLA_DK)
    vh = heads(v, GLA_DV)
    la_fh = heads(la_f, GLA_DK)
    la_bh = heads(la_b, GLA_DK)
    flip = lambda t: jnp.flip(t, axis=2)
    o_fwd = gla_chunked(qh, kh, vh, la_fh, strict=False)
    o_bwd = flip(gla_chunked(flip(qh), flip(kh), flip(vh), flip(la_bh), strict=True))
    o = rms_norm(o_fwd + o_bwd, head_norm)
    o = o.transpose(0, 2, 1, 3).reshape(b_, s_, GLA_VDIM).astype(h.dtype)
    y_gla = o * jax.nn.silu(g)
    return jnp.concatenate([y_conv, y_gla], axis=-1) @ w_out


def rope_angles(positions):
    inv_freq = ROPE_THETA ** (-jnp.arange(0, MLA_ROPE, 2, dtype=jnp.float32) / MLA_ROPE)
    ang = positions.astype(jnp.float32)[..., None] * inv_freq
    return jnp.cos(ang), jnp.sin(ang)


def apply_rope(t, cos, sin):
    t1, t2 = jnp.split(t, 2, axis=-1)
    return jnp.concatenate([t1 * cos - t2 * sin, t1 * sin + t2 * cos], axis=-1).astype(t.dtype)


def mla_mixer(h, positions, w_down, q_norm, kv_norm, w_uq, w_ukv, w_out):
    b_, s_, _ = h.shape
    c_q, c_kv, k_rope = jnp.split(h @ w_down, [MLA_Q_LORA, MLA_Q_LORA + MLA_KV_LORA], axis=-1)
    q = (rms_norm(c_q, q_norm) @ w_uq).reshape(b_, s_, MLA_HEADS, MLA_NOPE + MLA_ROPE)
    kv = (rms_norm(c_kv, kv_norm) @ w_ukv).reshape(b_, s_, MLA_HEADS, MLA_NOPE + MLA_V)
    q_nope, q_rope = q[..., :MLA_NOPE], q[..., MLA_NOPE:]
    k_nope, v = kv[..., :MLA_NOPE], kv[..., MLA_NOPE:]
    cos, sin = rope_angles(positions)
    q_rope = apply_rope(q_rope, cos[:, :, None, :], sin[:, :, None, :])
    k_rope = apply_rope(k_rope, cos, sin)
    scale = (MLA_NOPE + MLA_ROPE) ** -0.5
    nb = s_ // Q_BLOCK
    qn_blocks = q_nope.reshape(b_, nb, Q_BLOCK, MLA_HEADS, MLA_NOPE).transpose(1, 0, 2, 3, 4)
    qr_blocks = q_rope.reshape(b_, nb, Q_BLOCK, MLA_HEADS, MLA_ROPE).transpose(1, 0, 2, 3, 4)

    def attend(blk):
        qn, qr = blk
        s = (jnp.einsum('bqhd,bkhd->bhqk', qn, k_nope)
             + jnp.einsum('bqhd,bkd->bhqk', qr, k_rope)).astype(jnp.float32) * scale
        p = jax.nn.softmax(s, axis=-1).astype(v.dtype)
        return jnp.einsum('bhqk,bkhd->bqhd', p, v)

    o = lax.map(attend, (qn_blocks, qr_blocks))
    o = o.transpose(1, 0, 2, 3, 4).reshape(b_, s_, MLA_HEADS * MLA_V)
    return o @ w_out


def expert_choice_ffn(h, w_router, w_gate, w_up, w_down):
    b_, s_, d_ = h.shape
    cap = CAPACITY_FACTOR * s_ // N_EXPERTS
    affinity = jax.nn.softmax((h @ w_router).astype(jnp.float32), axis=-1)
    gates, idx = lax.top_k(affinity.transpose(0, 2, 1), cap)
    xg = jax.vmap(lambda hb, ib: hb[ib])(h, idx)
    hid = jax.nn.silu(jnp.einsum('becd,edf->becf', xg, w_gate)) * jnp.einsum('becd,edf->becf', xg, w_up)
    y = jnp.einsum('becf,efd->becd', hid, w_down) * gates[..., None].astype(h.dtype)
    flat_idx = (idx + jnp.arange(b_, dtype=idx.dtype)[:, None, None] * s_).reshape(-1)
    out = jax.ops.segment_sum(y.reshape(-1, d_), flat_idx, num_segments=b_ * s_)
    return out.reshape(b_, s_, d_)


def _normal(key, shape, scale):
    return jax.random.normal(key, shape, jnp.float32) * scale


def _gain(key, shape):
    return 1.0 + 0.02 * jax.random.normal(key, shape, jnp.float32)


def setup_inputs(seed: int = 0) -> dict:
    key = jax.random.key(seed)
    ks = jax.random.split(key, 24)
    na, nm, D = N_AB_LAYERS, N_MLA_LAYERS, D_MODEL
    x = jax.random.normal(ks[0], (BATCH, SEQ, D), jnp.float32)
    offset = jax.random.randint(ks[1], (BATCH, 1), 0, 4096, dtype=jnp.int32)
    positions = jnp.arange(SEQ, dtype=jnp.int32)[None, :] + offset
    return {
        "x": x,
        "positions": positions,
        "ab_norm": _gain(ks[2], (na, D)),
        "ab_w_in": _normal(ks[3], (na, D, AB_IN_DIM), D ** -0.5),
        "ab_conv_w": _normal(ks[4], (na, CONV_TAPS, CONV_WIDTH), CONV_TAPS ** -0.5),
        "gla_wa2_fwd": _normal(ks[5], (na, GLA_GATE_RANK, GLA_KDIM), GLA_GATE_RANK ** -0.5),
        "gla_ba_fwd": _normal(ks[6], (na, GLA_KDIM), 0.1),
        "gla_wa2_bwd": _normal(ks[7], (na, GLA_GATE_RANK, GLA_KDIM), GLA_GATE_RANK ** -0.5),
        "gla_ba_bwd": _normal(ks[8], (na, GLA_KDIM), 0.1),
        "gla_head_norm": _gain(ks[9], (na, GLA_DV)),
        "ab_w_out": _normal(ks[10], (na, AB_OUT_IN, D), AB_OUT_IN ** -0.5 * OUT_SCALE),
        "mla_norm": _gain(ks[11], (nm, D)),
        "mla_w_down": _normal(ks[12], (nm, D, MLA_DOWN_DIM), D ** -0.5),
        "mla_q_norm": _gain(ks[13], (nm, MLA_Q_LORA)),
        "mla_kv_norm": _gain(ks[14], (nm, MLA_KV_LORA)),
        "mla_w_uq": _normal(ks[15], (nm, MLA_Q_LORA, MLA_HEADS * (MLA_NOPE + MLA_ROPE)), MLA_Q_LORA ** -0.5),
        "mla_w_ukv": _normal(ks[16], (nm, MLA_KV_LORA, MLA_HEADS * (MLA_NOPE + MLA_V)), MLA_KV_LORA ** -0.5),
        "mla_w_out": _normal(ks[17], (nm, MLA_HEADS * MLA_V, D), (MLA_HEADS * MLA_V) ** -0.5 * OUT_SCALE),
        "ffn_norm": _gain(ks[18], (DEPTH, D)),
        "router_w": _normal(ks[19], (DEPTH, D, N_EXPERTS), D ** -0.5),
        "expert_w_gate": _normal(ks[20], (DEPTH, N_EXPERTS, D, EXPERT_FF), D ** -0.5),
        "expert_w_up": _normal(ks[21], (DEPTH, N_EXPERTS, D, EXPERT_FF), D ** -0.5),
        "expert_w_down": _normal(ks[22], (DEPTH, N_EXPERTS, EXPERT_FF, D), EXPERT_FF ** -0.5 * OUT_SCALE),
        "final_norm": _gain(ks[23], (D,)),
    }


def reference(x, positions, ab_norm, ab_w_in, ab_conv_w, gla_wa2_fwd, gla_ba_fwd,
              gla_wa2_bwd, gla_ba_bwd, gla_head_norm, ab_w_out,
              mla_norm, mla_w_down, mla_q_norm, mla_kv_norm, mla_w_uq, mla_w_ukv, mla_w_out,
              ffn_norm, router_w, expert_w_gate, expert_w_up, expert_w_down, final_norm):
    h = x
    for i in range(DEPTH):
        j = i // 2
        if i % 2 == 0:
            h = h + conv_gla_mixer(rms_norm(h, ab_norm[j]), ab_w_in[j], ab_conv_w[j],
                                   gla_wa2_fwd[j], gla_ba_fwd[j], gla_wa2_bwd[j], gla_ba_bwd[j],
                                   gla_head_norm[j], ab_w_out[j])
        else:
            h = h + mla_mixer(rms_norm(h, mla_norm[j]), positions, mla_w_down[j], mla_q_norm[j],
                              mla_kv_norm[j], mla_w_uq[j], mla_w_ukv[j], mla_w_out[j])
        h = h + expert_choice_ffn(rms_norm(h, ffn_norm[i]), router_w[i], expert_w_gate[i],
                                  expert_w_up[i], expert_w_down[i])
    return rms_norm(h, final_norm)
```

```python
import functools
import math

import jax
import jax.numpy as jnp
from jax import lax
from jax.experimental import pallas as pl
from jax.experimental.pallas import tpu as pltpu

F32 = jnp.float32
BF16 = jnp.bfloat16
I32 = jnp.int32

EPS = 1e-6
LANES = 128
VMEM_LIMIT_BYTES = 56 * 1024 * 1024
GLA_HEADS = 8
GLA_GATE_RANK = 16
GLA_GATE_TAU = 16.0
GLA_CHUNK = 64
MLA_NOPE = 128
MLA_ROPE = 64
MLA_V = 128
MLA_QK_PAD = 256
ROPE_THETA = 10000.0
N_EXPERTS = 16
CAPACITY_FACTOR = 2

NT_DIMS = (((1,), (1,)), ((), ()))
TN_DIMS = (((0,), (0,)), ((), ()))


def _cparams(*sem):
    return pltpu.CompilerParams(dimension_semantics=sem, vmem_limit_bytes=VMEM_LIMIT_BYTES)


def _tile(n, pref):
    t = min(n, pref)
    assert n % t == 0, (n, pref)
    return t


def _dot(a, b):
    return jnp.dot(a, b, preferred_element_type=F32)


def _split3(a):
    p0 = a.astype(BF16)
    r0 = a - p0.astype(F32)
    p1 = r0.astype(BF16)
    p2 = (r0 - p1.astype(F32)).astype(BF16)
    return p0, p1, p2


def _dot_exact_lhs(a, b01):
    p0, p1, p2 = _split3(a)
    return (_dot(p0, b01) + _dot(p1, b01)) + _dot(p2, b01)


def _rms(x, gain):
    return x * lax.rsqrt(jnp.mean(x * x, axis=-1, keepdims=True) + EPS) * gain


def _norm_mm_kernel(h_ref, g_ref, w_ref, o_ref, xn_ref):
    @pl.when(pl.program_id(1) == 0)
    def _():
        xn_ref[...] = _rms(h_ref[...], g_ref[...]).astype(BF16)

    o_ref[...] = _dot(xn_ref[...], w_ref[...]).astype(o_ref.dtype)


def norm_matmul(h, gain, w, out_dtype, tm=512, tn=512):
    s, d = h.shape
    n = w.shape[1]
    tm, tn = _tile(s, tm), _tile(n, tn)
    return pl.pallas_call(
        _norm_mm_kernel,
        grid=(s // tm, n // tn),
        in_specs=[pl.BlockSpec((tm, d), lambda i, j: (i, 0)),
                  pl.BlockSpec((1, d), lambda i, j: (0, 0)),
                  pl.BlockSpec((d, tn), lambda i, j: (0, j))],
        out_specs=pl.BlockSpec((tm, tn), lambda i, j: (i, j)),
        out_shape=jax.ShapeDtypeStruct((s, n), out_dtype),
        scratch_shapes=[pltpu.VMEM((tm, d), BF16)],
        compiler_params=_cparams("parallel", "arbitrary"),
        name="norm_matmul",
    )(h, gain.reshape(1, d), w)


def _mm_res_kernel(*refs, n_in):
    a_refs, w_refs = refs[:n_in], refs[n_in:2 * n_in]
    res_ref, o_ref = refs[2 * n_in], refs[2 * n_in + 1]
    acc = res_ref[...]
    for a, w in zip(a_refs, w_refs):
        acc = acc + _dot(a[...], w[...])
    o_ref[...] = acc


def matmul_residual(a_list, w_list, res, tm=512, tn=512):
    s, n = res.shape
    tm, tn = _tile(s, tm), _tile(n, tn)
    in_specs = [pl.BlockSpec((tm, a.shape[1]), lambda i, j: (i, 0)) for a in a_list]
    in_specs += [pl.BlockSpec((w.shape[0], tn), lambda i, j: (0, j)) for w in w_list]
    in_specs += [pl.BlockSpec((tm, tn), lambda i, j: (i, j))]
    return pl.pallas_call(
        functools.partial(_mm_res_kernel, n_in=len(a_list)),
        grid=(s // tm, n // tn),
        in_specs=in_specs,
        out_specs=pl.BlockSpec((tm, tn), lambda i, j: (i, j)),
        out_shape=jax.ShapeDtypeStruct((s, n), F32),
        compiler_params=_cparams("parallel", "arbitrary"),
        name="matmul_residual",
    )(*a_list, *w_list, res)


HALO = 16


def _conv_kernel(b_ref, c_ref, u_ref, cp_ref, up_ref, cn_ref, un_ref, w_ref, o_ref):
    i, nt = pl.program_id(0), pl.num_programs(0)
    tm = c_ref.shape[0]
    z = c_ref[...].astype(F32) * u_ref[...].astype(F32)
    zp = cp_ref[HALO - 1:HALO, :].astype(F32) * up_ref[HALO - 1:HALO, :].astype(F32)
    zn = cn_ref[0:1, :].astype(F32) * un_ref[0:1, :].astype(F32)
    zp = jnp.where(i > 0, zp, 0.0)
    zn = jnp.where(i < nt - 1, zn, 0.0)
    rows = lax.broadcasted_iota(I32, z.shape, 0)
    z_prev = jnp.where(rows == 0, zp, pltpu.roll(z, 1, axis=0))
    z_next = jnp.where(rows == tm - 1, zn, pltpu.roll(z, tm - 1, axis=0))
    w = w_ref[...]
    y = w[0:1] * z_prev + w[1:2] * z + w[2:3] * z_next
    o_ref[...] = (b_ref[...].astype(F32) * y).astype(o_ref.dtype)


def gated_conv(proj, conv_w, width, tm=512, tc=512):
    s = proj.shape[0]
    tm, tc = _tile(s, tm), _tile(width, tc)
    nc = width // tc
    nh = s // HALO
    per = tm // HALO

    def prev_map(off):
        return lambda i, c: (jnp.maximum(i * per - 1, 0), off * nc + c)

    def next_map(off):
        return lambda i, c: (jnp.minimum((i + 1) * per, nh - 1), off * nc + c)

    return pl.pallas_call(
        _conv_kernel,
        grid=(s // tm, nc),
        in_specs=[pl.BlockSpec((tm, tc), lambda i, c: (i, c)),
                  pl.BlockSpec((tm, tc), lambda i, c: (i, nc + c)),
                  pl.BlockSpec((tm, tc), lambda i, c: (i, 2 * nc + c)),
                  pl.BlockSpec((HALO, tc), prev_map(1)),
                  pl.BlockSpec((HALO, tc), prev_map(2)),
                  pl.BlockSpec((HALO, tc), next_map(1)),
                  pl.BlockSpec((HALO, tc), next_map(2)),
                  pl.BlockSpec((3, tc), lambda i, c: (0, c))],
        out_specs=pl.BlockSpec((tm, tc), lambda i, c: (i, c)),
        out_shape=jax.ShapeDtypeStruct((s, width), BF16),
        compiler_params=_cparams("parallel", "parallel"),
        name="gated_conv",
    )(proj, proj, proj, proj, proj, proj, proj, conv_w)


def _log_sigmoid(z):
    return -(jnp.maximum(-z, 0.0) + jnp.log1p(jnp.exp(-jnp.abs(z))))


def _gla_log_decay(a_ref, wa_ref, ba_ref):
    a = a_ref[...]
    w = wa_ref[...]
    a_hi = a.astype(BF16)
    a_lo = (a - a_hi.astype(F32)).astype(BF16)
    w_hi = w.astype(BF16)
    w_lo = (w - w_hi.astype(F32)).astype(BF16)
    z = (_dot(a_hi, w_hi) + _dot(a_lo, w_hi)) + _dot(a_hi, w_lo) + ba_ref[...]
    return _log_sigmoid(z) / GLA_GATE_TAU


def _gla_chunk(q, k, v, la, state_t, reverse):
    n = q.shape[0]
    r = lax.broadcasted_iota(I32, (n, n), 0)
    c = lax.broadcasted_iota(I32, (n, n), 1)
    tri = jnp.where((c >= r) if reverse else (c <= r), 1.0, 0.0).astype(BF16)
    la_hi = la.astype(BF16)
    la_lo = (la - la_hi.astype(F32)).astype(BF16)
    cum = _dot(tri, la_hi) + _dot(tri, la_lo)
    cum_last = cum[0:1] if reverse else cum[n - 1:n]
    qd = (q * jnp.exp(cum)).astype(BF16)
    kd = (k * jnp.exp(-cum)).astype(BF16)
    ke = (k * jnp.exp(cum_last - cum)).astype(BF16)
    dec = jnp.exp(cum_last)
    sc = lax.dot_general(qd, kd, NT_DIMS, preferred_element_type=F32)
    sc = jnp.where((c > r) if reverse else (c <= r), sc, 0.0).astype(BF16)
    o = _dot(sc, v) + lax.dot_general(qd, state_t.astype(BF16), NT_DIMS, preferred_element_type=F32)
    new_state = state_t * dec + lax.dot_general(v, ke, TN_DIMS, preferred_element_type=F32)
    return o, new_state


def _gla_scan_tile(q_ref, k_ref, v_ref, la, st_ref, reverse, emit):
    tc = q_ref.shape[0]
    nchunk = tc // GLA_CHUNK
    scale = q_ref.shape[1] ** -0.5
    order = range(nchunk - 1, -1, -1) if reverse else range(nchunk)
    state = st_ref[...]
    for ci in order:
        sl = slice(ci * GLA_CHUNK, (ci + 1) * GLA_CHUNK)
        q = q_ref[sl, :].astype(F32) * scale
        k = k_ref[sl, :].astype(F32)
        o, state = _gla_chunk(q, k, v_ref[sl, :], la[sl, :], state, reverse)
        emit(sl, o)
    st_ref[...] = state


def _gla_fwd_kernel(q_ref, k_ref, v_ref, a_ref, wa_ref, ba_ref, o_ref, st_ref):
    @pl.when(pl.program_id(1) == 0)
    def _():
        st_ref[...] = jnp.zeros_like(st_ref)

    la = _gla_log_decay(a_ref, wa_ref, ba_ref)

    def emit(sl, o):
        o_ref[sl, :] = o

    _gla_scan_tile(q_ref, k_ref, v_ref, la, st_ref, False, emit)


def _gla_bwd_kernel(q_ref, k_ref, v_ref, a_ref, wa_ref, ba_ref, of_ref, g_ref, hn_ref, y_ref, st_ref):
    @pl.when(pl.program_id(1) == 0)
    def _():
        st_ref[...] = jnp.zeros_like(st_ref)

    la = _gla_log_decay(a_ref, wa_ref, ba_ref)

    def emit(sl, o):
        tot = _rms(of_ref[sl, :] + o, hn_ref[...])
        g = g_ref[sl, :].astype(F32)
        y_ref[sl, :] = (tot * (g * jax.nn.sigmoid(g))).astype(y_ref.dtype)

    _gla_scan_tile(q_ref, k_ref, v_ref, la, st_ref, True, emit)


def gla_mixer(proj, a_lr, wa_f, ba_f, wa_b, ba_b, head_norm, col_q, col_k, col_v, col_g, dk, dv, tc=512):
    s = proj.shape[0]
    tc = _tile(s, tc)
    nt = s // tc
    h_ = GLA_HEADS
    lr = a_lr.shape[1]

    def specs(row):
        return [pl.BlockSpec((tc, dk), lambda h, i: (row(i), col_q // dk + h)),
                pl.BlockSpec((tc, dk), lambda h, i: (row(i), col_k // dk + h)),
                pl.BlockSpec((tc, dv), lambda h, i: (row(i), col_v // dv + h)),
                pl.BlockSpec((tc, lr), lambda h, i: (row(i), 0)),
                pl.BlockSpec((lr, dk), lambda h, i: (0, h)),
                pl.BlockSpec((1, dk), lambda h, i: (0, h))]

    fwd_row = lambda i: i
    o_fwd = pl.pallas_call(
        _gla_fwd_kernel,
        grid=(h_, nt),
        in_specs=specs(fwd_row),
        out_specs=pl.BlockSpec((tc, dv), lambda h, i: (i, h)),
        out_shape=jax.ShapeDtypeStruct((s, h_ * dv), F32),
        scratch_shapes=[pltpu.VMEM((dv, dk), F32)],
        compiler_params=_cparams("parallel", "arbitrary"),
        name="gla_fwd",
    )(proj, proj, proj, a_lr, wa_f, ba_f)

    bwd_row = lambda i: nt - 1 - i
    return pl.pallas_call(
        _gla_bwd_kernel,
        grid=(h_, nt),
        in_specs=specs(bwd_row) + [
            pl.BlockSpec((tc, dv), lambda h, i: (bwd_row(i), h)),
            pl.BlockSpec((tc, dv), lambda h, i: (bwd_row(i), col_g // dv + h)),
            pl.BlockSpec((1, dv), lambda h, i: (0, 0))],
        out_specs=pl.BlockSpec((tc, dv), lambda h, i: (bwd_row(i), h)),
        out_shape=jax.ShapeDtypeStruct((s, h_ * dv), BF16),
        scratch_shapes=[pltpu.VMEM((dv, dk), F32)],
        compiler_params=_cparams("parallel", "arbitrary"),
        name="gla_bwd",
    )(proj, proj, proj, a_lr, wa_b, ba_b, o_fwd, proj, head_norm)


def _rope_tables_kernel(pos_ref, invf_ref, cos_ref, sina_ref, sinb_ref):
    ang = pos_ref[...].astype(F32) * invf_ref[...]
    lane = lax.broadcasted_iota(I32, ang.shape, 1)
    half = MLA_ROPE // 2
    c, s_ = jnp.cos(ang), jnp.sin(ang)
    cos_ref[...] = jnp.where(lane < 2 * half, c, 0.0)
    sina_ref[...] = jnp.where(lane < half, -s_, 0.0)
    sinb_ref[...] = jnp.where((lane >= half) & (lane < 2 * half), s_, 0.0)


def rope_tables(positions, tm=512):
    s = positions.shape[0]
    tm = _tile(s, tm)
    half = MLA_ROPE // 2
    inv_freq = ROPE_THETA ** (-jnp.arange(0, MLA_ROPE, 2, dtype=F32) / MLA_ROPE)
    invf = jnp.concatenate([inv_freq, inv_freq, jnp.zeros((LANES - 2 * half,), F32)]).reshape(1, LANES)
    spec = pl.BlockSpec((tm, LANES), lambda i: (i, 0))
    return pl.pallas_call(
        _rope_tables_kernel,
        grid=(s // tm,),
        in_specs=[pl.BlockSpec((tm, 1), lambda i: (i, 0)), pl.BlockSpec((1, LANES), lambda i: (0, 0))],
        out_specs=[spec, spec, spec],
        out_shape=[jax.ShapeDtypeStruct((s, LANES), F32)] * 3,
        compiler_params=_cparams("parallel"),
        name="rope_tables",
    )(positions.reshape(s, 1), invf)


def _rope(x, cos_t, sin_a, sin_b):
    return x * cos_t + pltpu.roll(x, LANES - MLA_ROPE // 2, axis=1) * sin_a + pltpu.roll(x, MLA_ROPE // 2, axis=1) * sin_b


def _mla_post_kernel(d_ref, qn_ref, kvn_ref, cos_ref, sina_ref, sinb_ref, cq_ref, ckv_ref, kr_ref, *, q_lora, kv_lora):
    d = d_ref[...]
    cq_ref[...] = _rms(d[:, :q_lora], qn_ref[...]).astype(BF16)
    ckv_ref[...] = _rms(d[:, q_lora:q_lora + kv_lora], kvn_ref[...]).astype(BF16)
    kr = d[:, q_lora + kv_lora:]
    kr_ref[...] = _rope(kr, cos_ref[...], sina_ref[...], sinb_ref[...]).astype(BF16)


def mla_post(down, q_norm, kv_norm, tables, q_lora, kv_lora, tm=512):
    s, n = down.shape
    tm = _tile(s, tm)
    tab = pl.BlockSpec((tm, LANES), lambda i: (i, 0))
    return pl.pallas_call(
        functools.partial(_mla_post_kernel, q_lora=q_lora, kv_lora=kv_lora),
        grid=(s // tm,),
        in_specs=[pl.BlockSpec((tm, n), lambda i: (i, 0)),
                  pl.BlockSpec((1, q_lora), lambda i: (0, 0)),
                  pl.BlockSpec((1, kv_lora), lambda i: (0, 0)), tab, tab, tab],
        out_specs=[pl.BlockSpec((tm, q_lora), lambda i: (i, 0)),
                   pl.BlockSpec((tm, kv_lora), lambda i: (i, 0)),
                   pl.BlockSpec((tm, LANES), lambda i: (i, 0))],
        out_shape=[jax.ShapeDtypeStruct((s, q_lora), BF16),
                   jax.ShapeDtypeStruct((s, kv_lora), BF16),
                   jax.ShapeDtypeStruct((s, LANES), BF16)],
        compiler_params=_cparams("parallel"),
        name="mla_post",
    )(down, q_norm.reshape(1, -1), kv_norm.reshape(1, -1), *tables)


def _q_up_kernel(x_ref, w_ref, cos_ref, sina_ref, sinb_ref, q_ref, *, qscale):
    a = _dot(x_ref[...], w_ref[0])
    rot = _rope(a[:, MLA_NOPE:], cos_ref[...], sina_ref[...], sinb_ref[...])
    q_ref[0] = (jnp.concatenate([a[:, :MLA_NOPE], rot], axis=1) * qscale).astype(BF16)


def mla_q_up(cq, w_uq_h, tables, qscale, tm=512):
    s, r = cq.shape
    h_ = w_uq_h.shape[0]
    tm = _tile(s, tm)
    tab = pl.BlockSpec((tm, LANES), lambda i, h: (i, 0))
    return pl.pallas_call(
        functools.partial(_q_up_kernel, qscale=qscale),
        grid=(s // tm, h_),
        in_specs=[pl.BlockSpec((tm, r), lambda i, h: (i, 0)),
                  pl.BlockSpec((1, r, MLA_QK_PAD), lambda i, h: (h, 0, 0)), tab, tab, tab],
        out_specs=pl.BlockSpec((1, tm, MLA_QK_PAD), lambda i, h: (h, i, 0)),
        out_shape=jax.ShapeDtypeStruct((h_, s, MLA_QK_PAD), BF16),
        compiler_params=_cparams("parallel", "arbitrary"),
        name="mla_q_up",
    )(cq, w_uq_h, *tables)


def _kv_up_kernel(x_ref, w_ref, kr_ref, k_ref, v_ref):
    a = _dot(x_ref[...], w_ref[0])
    k_ref[0] = jnp.concatenate([a[:, :MLA_NOPE].astype(BF16), kr_ref[...]], axis=1)
    v_ref[0] = a[:, MLA_NOPE:].astype(BF16)


def mla_kv_up(ckv, w_ukv_h, kr, tm=512):
    s, r = ckv.shape
    h_ = w_ukv_h.shape[0]
    tm = _tile(s, tm)
    return pl.pallas_call(
        _kv_up_kernel,
        grid=(s // tm, h_),
        in_specs=[pl.BlockSpec((tm, r), lambda i, h: (i, 0)),
                  pl.BlockSpec((1, r, MLA_NOPE + MLA_V), lambda i, h: (h, 0, 0)),
                  pl.BlockSpec((tm, LANES), lambda i, h: (i, 0))],
        out_specs=[pl.BlockSpec((1, tm, MLA_QK_PAD), lambda i, h: (h, i, 0)),
                   pl.BlockSpec((1, tm, MLA_V), lambda i, h: (h, i, 0))],
        out_shape=[jax.ShapeDtypeStruct((h_, s, MLA_QK_PAD), BF16),
                   jax.ShapeDtypeStruct((h_, s, MLA_V), BF16)],
        compiler_params=_cparams("parallel", "arbitrary"),
        name="mla_kv_up",
    )(ckv, w_ukv_h, kr)


def _attn_kernel(q_ref, k_ref, v_ref, o_ref, m_ref, l_ref, acc_ref):
    kv = pl.program_id(2)

    @pl.when(kv == 0)
    def _():
        m_ref[...] = jnp.full_like(m_ref, -jnp.inf)
        l_ref[...] = jnp.zeros_like(l_ref)
        acc_ref[...] = jnp.zeros_like(acc_ref)

    s = lax.dot_general(q_ref[0], k_ref[0], NT_DIMS, preferred_element_type=F32)
    m_prev = m_ref[...]
    m_new = jnp.maximum(m_prev, jnp.max(s, axis=1, keepdims=True))
    alpha = jnp.exp2(m_prev - m_new)
    p = jnp.exp2(s - m_new)
    l_ref[...] = alpha * l_ref[...] + jnp.sum(p, axis=1, keepdims=True)
    acc_ref[...] = alpha * acc_ref[...] + _dot(p.astype(BF16), v_ref[0])
    m_ref[...] = m_new

    @pl.when(kv == pl.num_programs(2) - 1)
    def _():
        o_ref[...] = (acc_ref[...] / l_ref[...]).astype(o_ref.dtype)


def mla_attention(q, k, v, tq=512, tk=1024):
    h_, s, dq = q.shape
    dv = v.shape[2]
    tq, tk = _tile(s, tq), _tile(s, tk)
    return pl.pallas_call(
        _attn_kernel,
        grid=(h_, s // tq, s // tk),
        in_specs=[pl.BlockSpec((1, tq, dq), lambda h, i, j: (h, i, 0)),
                  pl.BlockSpec((1, tk, dq), lambda h, i, j: (h, j, 0)),
                  pl.BlockSpec((1, tk, dv), lambda h, i, j: (h, j, 0))],
        out_specs=pl.BlockSpec((tq, dv), lambda h, i, j: (i, h)),
        out_shape=jax.ShapeDtypeStruct((s, h_ * dv), BF16),
        scratch_shapes=[pltpu.VMEM((tq, 1), F32), pltpu.VMEM((tq, 1), F32), pltpu.VMEM((tq, dv), F32)],
        compiler_params=_cparams("parallel", "parallel", "arbitrary"),
        name="mla_attention",
    )(q, k, v)


def _router_kernel(h_ref, g_ref, wt_ref, aff_ref):
    xn = _rms(h_ref[...], g_ref[...])
    x0, x1, x2 = _split3(xn)
    w0, w1, w2 = _split3(wt_ref[...])

    def nt(a, b):
        return lax.dot_general(a, b, NT_DIMS, preferred_element_type=F32)

    logits = (((nt(w2, x0) + nt(w0, x2)) + nt(w1, x1)) + (nt(w1, x0) + nt(w0, x1))) + nt(w0, x0)
    e = jnp.exp(logits - jnp.max(logits, axis=0, keepdims=True))
    aff_ref[...] = e / jnp.sum(e, axis=0, keepdims=True)


def router_affinity(h, gain, w_router_t, tm=512):
    s, d = h.shape
    e_ = w_router_t.shape[0]
    tm = _tile(s, tm)
    return pl.pallas_call(
        _router_kernel,
        grid=(s // tm,),
        in_specs=[pl.BlockSpec((tm, d), lambda i: (i, 0)),
                  pl.BlockSpec((1, d), lambda i: (0, 0)),
                  pl.BlockSpec((e_, d), lambda i: (0, 0))],
        out_specs=pl.BlockSpec((e_, tm), lambda i: (0, i)),
        out_shape=jax.ShapeDtypeStruct((e_, s), F32),
        compiler_params=_cparams("parallel"),
        name="router_affinity",
    )(h, gain.reshape(1, d), w_router_t)


def _prefix_parts(xb, u_incl, ones_b, sl_strict):
    loc = _dot(xb, u_incl)
    tot = _dot(xb, ones_b)
    base = _dot(sl_strict, tot.astype(BF16))
    return loc, tot, base


def _select_kernel(aff_ref, idx_ref, gate_ref, rank_ref, start_ref, end_ref, sel_ref, *, cap):
    e_, nt, ln = aff_ref.shape
    li = lax.broadcasted_iota(I32, (ln, ln), 0)
    lj = lax.broadcasted_iota(I32, (ln, ln), 1)
    u_incl = jnp.where(li <= lj, 1.0, 0.0).astype(BF16)
    u_incl_t = jnp.where(lj <= li, 1.0, 0.0).astype(BF16)
    ones_b = jnp.ones((ln, ln), BF16)
    ti = lax.broadcasted_iota(I32, (nt, nt), 0)
    tj = lax.broadcasted_iota(I32, (nt, nt), 1)
    sl_strict = jnp.where(tj < ti, 1.0, 0.0).astype(BF16)

    bits = pltpu.bitcast(aff_ref[...], I32)

    def count_ge(thr):
        ge = jnp.where(bits >= thr, 1.0, 0.0)
        return jnp.sum(jnp.sum(ge, axis=2, keepdims=True), axis=1, keepdims=True)

    def search(b, thr):
        cand = thr | jnp.left_shift(jnp.int32(1), 30 - b)
        return jnp.where(count_ge(cand) >= cap, cand, thr)

    thr = lax.fori_loop(0, 31, search, jnp.zeros((e_, 1, 1), I32))
    gt = jnp.where(bits > thr, 1.0, 0.0)
    eq = jnp.where(bits == thr, 1.0, 0.0)
    need = cap - jnp.sum(jnp.sum(gt, axis=2, keepdims=True), axis=1, keepdims=True)

    cnt = jnp.zeros((nt, ln), F32)
    for e in range(e_):
        loc, _, base = _prefix_parts(eq[e].astype(BF16), u_incl, ones_b, sl_strict)
        eq_rank = loc + base - eq[e]
        sel = gt[e] + eq[e] * jnp.where(eq_rank < need[e], 1.0, 0.0)
        sel_ref[e] = sel
        cnt = cnt + sel

    cnt_b = cnt.astype(BF16)
    loc_c = _dot(cnt_b, u_incl)
    tot_c = _dot(cnt_b, ones_b)
    t0, t1, t2 = _split3(tot_c)
    base_c = (_dot(sl_strict, t0) + _dot(sl_strict, t1)) + _dot(sl_strict, t2)
    end = loc_c + base_c
    start = end - cnt
    start_ref[...] = start.astype(I32)
    end_ref[...] = end.astype(I32)

    p_row = lax.broadcasted_iota(I32, (1, cap), 1).astype(F32)
    tile_iota = lax.broadcasted_iota(I32, (nt, cap), 0).astype(F32)
    lane_iota = lax.broadcasted_iota(I32, (ln, cap), 0).astype(F32)

    def compact(e, within):
        sel = sel_ref[e]
        selb = sel.astype(BF16)
        _, tot, base = _prefix_parts(selb, u_incl, ones_b, sl_strict)
        tile_end = (base + tot)[:, 0:1]
        jp = jnp.sum(jnp.where(tile_end <= p_row, 1.0, 0.0), axis=0, keepdims=True)
        g_t = jnp.where(tile_iota == jp, 1.0, 0.0)
        base_p = jnp.sum(g_t * base[:, 0:1], axis=0, keepdims=True)
        r_p = p_row - base_p
        loc_t = lax.dot_general(u_incl_t, selb, NT_DIMS, preferred_element_type=F32)
        sel_loc = _dot(loc_t.astype(BF16), g_t.astype(BF16))
        lane_p = jnp.sum(jnp.where(sel_loc <= r_p, 1.0, 0.0), axis=0, keepdims=True)
        l_t = jnp.where(lane_iota == lane_p, 1.0, 0.0)
        l_b = l_t.astype(BF16)
        idx_ref[e] = (jp * ln + lane_p).astype(I32)
        gate_ref[e] = jnp.sum(g_t * _dot_exact_lhs(aff_ref[e], l_b), axis=0, keepdims=True)
        rank_full = start + within
        rank_ref[e] = jnp.sum(g_t * _dot_exact_lhs(rank_full, l_b), axis=0, keepdims=True).astype(I32)
        return within + sel

    lax.fori_loop(0, e_, compact, jnp.zeros((nt, ln), F32))


def expert_select(aff3, cap):
    e_, nt, ln = aff3.shape
    full3 = lambda a, b, c: pl.BlockSpec((a, b, c), lambda i: (0, 0, 0))
    full2 = lambda a, b: pl.BlockSpec((a, b), lambda i: (0, 0))
    return pl.pallas_call(
        functools.partial(_select_kernel, cap=cap),
        grid=(1,),
        in_specs=[full3(e_, nt, ln)],
        out_specs=[full3(e_, 1, cap), full3(e_, 1, cap), full3(e_, 1, cap), full2(nt, ln), full2(nt, ln)],
        out_shape=[jax.ShapeDtypeStruct((e_, 1, cap), I32),
                   jax.ShapeDtypeStruct((e_, 1, cap), F32),
                   jax.ShapeDtypeStruct((e_, 1, cap), I32),
                   jax.ShapeDtypeStruct((nt, ln), I32),
                   jax.ShapeDtypeStruct((nt, ln), I32)],
        scratch_shapes=[pltpu.VMEM((e_, nt, ln), F32)],
        compiler_params=_cparams("arbitrary"),
        name="expert_select",
    )(aff3)


def _row_gather_copy(h_hbm, xbuf, sem, slot, tok, p):
    return pltpu.make_async_copy(h_hbm.at[pl.ds(tok, 1)], xbuf.at[slot, pl.ds(p, 1)], sem.at[slot])


def _ffn_up_kernel(idx_ref, h_hbm, g_ref, wg_ref, wu_ref, hid_ref, xbuf, sem):
    tm = xbuf.shape[1]
    step = pl.program_id(0) * pl.num_programs(1) + pl.program_id(1)
    nsteps = pl.num_programs(0) * pl.num_programs(1)
    slot = step % 2

    def issue(s_, slot_):
        def body(p, carry):
            _row_gather_copy(h_hbm, xbuf, sem, slot_, idx_ref[s_ * tm + p], p).start()
            return carry
        lax.fori_loop(0, tm, body, 0)

    @pl.when(step == 0)
    def _():
        issue(step, slot)

    @pl.when(step + 1 < nsteps)
    def _():
        issue(step + 1, 1 - slot)

    pltpu.make_async_copy(h_hbm.at[pl.ds(0, tm)], xbuf.at[slot], sem.at[slot]).wait()
    xn = _rms(xbuf[slot], g_ref[...]).astype(BF16)
    gate = _dot(xn, wg_ref[0])
    up = _dot(xn, wu_ref[0])
    hid_ref[...] = (gate * jax.nn.sigmoid(gate) * up).astype(hid_ref.dtype)


def ffn_up(h, gain, idx_flat, w_gate, w_up, cap, tm=512):
    s, d = h.shape
    e_, _, ff = w_gate.shape
    tm = _tile(cap, tm)
    r = cap // tm
    grid_spec = pltpu.PrefetchScalarGridSpec(
        num_scalar_prefetch=1,
        grid=(e_, r),
        in_specs=[pl.BlockSpec(memory_space=pl.ANY),
                  pl.BlockSpec((1, d), lambda e, i, idx: (0, 0)),
                  pl.BlockSpec((1, d, ff), lambda e, i, idx: (e, 0, 0)),
                  pl.BlockSpec((1, d, ff), lambda e, i, idx: (e, 0, 0))],
        out_specs=pl.BlockSpec((tm, ff), lambda e, i, idx: (e * r + i, 0)),
        scratch_shapes=[pltpu.VMEM((2, tm, d), F32), pltpu.SemaphoreType.DMA((2,))],
    )
    return pl.pallas_call(
        _ffn_up_kernel,
        grid_spec=grid_spec,
        out_shape=jax.ShapeDtypeStruct((e_ * cap, ff), BF16),
        compiler_params=_cparams("arbitrary", "arbitrary"),
        name="ffn_up",
    )(idx_flat, h, gain.reshape(1, d), w_gate, w_up)


def _row_scatter_copy(ybuf, y_hbm, sem, slot, p, row):
    return pltpu.make_async_copy(ybuf.at[slot, pl.ds(p, 1)], y_hbm.at[pl.ds(row, 1)], sem.at[slot])


def _ffn_down_kernel(rank_ref, hid_ref, wd_ref, gate_ref, y_hbm, ybuf, sem):
    tm = ybuf.shape[1]
    step = pl.program_id(0) * pl.num_programs(1) + pl.program_id(1)
    nsteps = pl.num_programs(0) * pl.num_programs(1)
    slot = step % 2

    def drain(slot_):
        pltpu.make_async_copy(ybuf.at[slot_], y_hbm.at[pl.ds(0, tm)], sem.at[slot_]).wait()

    @pl.when(step >= 2)
    def _():
        drain(slot)

    ybuf[slot] = _dot(hid_ref[...], wd_ref[0]) * gate_ref[...]

    def body(p, carry):
        _row_scatter_copy(ybuf, y_hbm, sem, slot, p, rank_ref[step * tm + p]).start()
        return carry
    lax.fori_loop(0, tm, body, 0)

    @pl.when(step == nsteps - 1)
    def _():
        drain(slot)

        @pl.when(nsteps >= 2)
        def _():
            drain(1 - slot)


def ffn_down(hid, w_down, gate_col, rank_flat, cap, tm=512):
    n, ff = hid.shape
    e_, _, d = w_down.shape
    tm = _tile(cap, tm)
    r = cap // tm
    grid_spec = pltpu.PrefetchScalarGridSpec(
        num_scalar_prefetch=1,
        grid=(e_, r),
        in_specs=[pl.BlockSpec((tm, ff), lambda e, i, rk: (e * r + i, 0)),
                  pl.BlockSpec((1, ff, d), lambda e, i, rk: (e, 0, 0)),
                  pl.BlockSpec((tm, 1), lambda e, i, rk: (e * r + i, 0))],
        out_specs=pl.BlockSpec(memory_space=pl.ANY),
        scratch_shapes=[pltpu.VMEM((2, tm, d), F32), pltpu.SemaphoreType.DMA((2,))],
    )
    return pl.pallas_call(
        _ffn_down_kernel,
        grid_spec=grid_spec,
        out_shape=jax.ShapeDtypeStruct((n, d), F32),
        compiler_params=_cparams("arbitrary", "arbitrary"),
        name="ffn_down",
    )(rank_flat, hid, w_down, gate_col)


def _combine_kernel(ws_ref, h_ref, st_ref, en_ref, y_hbm, o_ref, win, sem, *, total):
    i = pl.program_id(0)
    wrows = win.shape[0]
    lo, hi = ws_ref[i], ws_ref[i + 1]
    first = (lo // 8) * 8
    nwin = (hi - first + wrows - 1) // wrows
    st, en = st_ref[...], en_ref[...]

    def body(k, acc):
        w = first + k * wrows
        w0 = pl.multiple_of(jnp.minimum(w, total - wrows), 8)
        cp = pltpu.make_async_copy(y_hbm.at[pl.ds(w0, wrows)], win, sem)
        cp.start()
        cp.wait()
        r_abs = w0 + lax.broadcasted_iota(I32, (1, wrows), 1)
        q = jnp.where((st <= r_abs) & (r_abs < en) & (r_abs >= w), 1.0, 0.0).astype(BF16)
        return acc + _dot(q, win[...].astype(BF16))

    acc = lax.fori_loop(0, nwin, body, jnp.zeros(o_ref.shape, F32))
    o_ref[...] = h_ref[...] + acc


def ffn_combine(h, y_sorted, start_col, end_col, win_starts, tm=256, wrows=256):
    s, d = h.shape
    total = y_sorted.shape[0]
    tm = _tile(s, tm)
    wrows = _tile(total, wrows)
    grid_spec = pltpu.PrefetchScalarGridSpec(
        num_scalar_prefetch=1,
        grid=(s // tm,),
        in_specs=[pl.BlockSpec((tm, d), lambda i, ws: (i, 0)),
                  pl.BlockSpec((tm, 1), lambda i, ws: (i, 0)),
                  pl.BlockSpec((tm, 1), lambda i, ws: (i, 0)),
                  pl.BlockSpec(memory_space=pl.ANY)],
        out_specs=pl.BlockSpec((tm, d), lambda i, ws: (i, 0)),
        scratch_shapes=[pltpu.VMEM((wrows, d), F32), pltpu.SemaphoreType.DMA(())],
    )
    return pl.pallas_call(
        functools.partial(_combine_kernel, total=total),
        grid_spec=grid_spec,
        out_shape=jax.ShapeDtypeStruct((s, d), F32),
        compiler_params=_cparams("arbitrary"),
        name="ffn_combine",
    )(win_starts, h, start_col, end_col, y_sorted)


def expert_choice_ffn(h, gain, w_router_t, w_gate, w_up, w_down, combine_tm=256):
    s, d = h.shape
    e_ = w_router_t.shape[0]
    cap = CAPACITY_FACTOR * s // e_
    aff = router_affinity(h, gain, w_router_t)
    idx, gate, rank, start, end = expert_select(aff.reshape(e_, s // LANES, LANES), cap)
    hid = ffn_up(h, gain, idx.reshape(-1), w_gate, w_up, cap)
    y_sorted = ffn_down(hid, w_down, gate.reshape(-1, 1), rank.reshape(-1), cap)
    start_flat = start.reshape(-1)
    tm = _tile(s, combine_tm)
    win_starts = jnp.concatenate([start_flat[::tm], jnp.full((1,), e_ * cap, I32)])
    return ffn_combine(h, y_sorted, start_flat.reshape(s, 1), end.reshape(s, 1), win_starts, tm=tm)


def _final_norm_kernel(h_ref, g_ref, o_ref):
    o_ref[...] = _rms(h_ref[...], g_ref[...])


def _final_norm_call(h, gain, tm=512):
    s, d = h.shape
    tm = _tile(s, tm)
    return pl.pallas_call(
        _final_norm_kernel,
        grid=(s // tm,),
        in_specs=[pl.BlockSpec((tm, d), lambda i: (i, 0)), pl.BlockSpec((1, d), lambda i: (0, 0))],
        out_specs=pl.BlockSpec((tm, d), lambda i: (i, 0)),
        out_shape=jax.ShapeDtypeStruct((s, d), F32),
        compiler_params=_cparams("parallel"),
        name="final_norm",
    )(h, gain.reshape(1, d))


def conv_gla_layer(h, norm, w_in, conv_w, wa2_f, ba_f, wa2_b, ba_b, head_norm, w_out):
    d = h.shape[1]
    width = conv_w.shape[1]
    kdim = wa2_f.shape[1]
    dk = kdim // GLA_HEADS
    dv = head_norm.shape[0]
    vdim = dv * GLA_HEADS
    n_main = 3 * width + 2 * kdim + 2 * vdim
    lr = 2 * GLA_GATE_RANK
    col_q = 3 * width
    col_k, col_v, col_g = col_q + kdim, col_q + 2 * kdim, col_q + 2 * kdim + vdim

    w_main = w_in[:, :n_main].astype(BF16)
    w_lr = jnp.pad(w_in[:, n_main:], ((0, 0), (0, LANES - lr))).astype(BF16)
    proj = norm_matmul(h, norm, w_main, BF16)
    a_lr = norm_matmul(h, norm, w_lr, F32, tn=LANES)
    wa_f = jnp.pad(wa2_f, ((0, LANES - GLA_GATE_RANK), (0, 0)))
    wa_b = jnp.pad(wa2_b, ((GLA_GATE_RANK, LANES - lr), (0, 0)))
    y_conv = gated_conv(proj, conv_w, width)
    y_gla = gla_mixer(proj, a_lr, wa_f, ba_f.reshape(1, -1), wa_b, ba_b.reshape(1, -1),
                      head_norm.reshape(1, -1), col_q, col_k, col_v, col_g, dk, dv)
    w_out_b = w_out.astype(BF16)
    return matmul_residual([y_conv, y_gla], [w_out_b[:width], w_out_b[width:]], h)


def mla_layer(h, tables, norm, w_down, q_norm, kv_norm, w_uq, w_ukv, w_out):
    q_lora, kv_lora = q_norm.shape[0], kv_norm.shape[0]
    heads = w_uq.shape[1] // (MLA_NOPE + MLA_ROPE)
    n_down = w_down.shape[1]
    n_pad = q_lora + kv_lora + LANES
    w_down_b = jnp.pad(w_down, ((0, 0), (0, n_pad - n_down))).astype(BF16)
    down = norm_matmul(h, norm, w_down_b, F32, tm=256, tn=n_pad)
    cq, ckv, kr = mla_post(down, q_norm, kv_norm, tables, q_lora, kv_lora)
    w_uq_h = w_uq.reshape(q_lora, heads, MLA_NOPE + MLA_ROPE).transpose(1, 0, 2)
    w_uq_h = jnp.pad(w_uq_h, ((0, 0), (0, 0), (0, MLA_QK_PAD - MLA_NOPE - MLA_ROPE))).astype(BF16)
    w_ukv_h = w_ukv.reshape(kv_lora, heads, MLA_NOPE + MLA_V).transpose(1, 0, 2).astype(BF16)
    qscale = (MLA_NOPE + MLA_ROPE) ** -0.5 * math.log2(math.e)
    q = mla_q_up(cq, w_uq_h, tables, qscale)
    k, v = mla_kv_up(ckv, w_ukv_h, kr)
    o = mla_attention(q, k, v)
    return matmul_residual([o], [w_out.astype(BF16)], h)


def kernel(x, positions, ab_norm, ab_w_in, ab_conv_w, gla_wa2_fwd, gla_ba_fwd, gla_wa2_bwd, gla_ba_bwd, gla_head_norm, ab_w_out, mla_norm, mla_w_down, mla_q_norm, mla_kv_norm, mla_w_uq, mla_w_ukv, mla_w_out, ffn_norm, router_w, expert_w_gate, expert_w_up, expert_w_down, final_norm):
    b_, s, d = x.shape
    depth = ffn_norm.shape[0]
    outs = []
    for b in range(b_):
        h = x[b]
        tables = rope_tables(positions[b])
        for i in range(depth):
            j = i // 2
            if i % 2 == 0:
                h = conv_gla_layer(h, ab_norm[j], ab_w_in[j], ab_conv_w[j], gla_wa2_fwd[j], gla_ba_fwd[j],
                                   gla_wa2_bwd[j], gla_ba_bwd[j], gla_head_norm[j], ab_w_out[j])
            else:
                h = mla_layer(h, tables, mla_norm[j], mla_w_down[j], mla_q_norm[j], mla_kv_norm[j],
                              mla_w_uq[j], mla_w_ukv[j], mla_w_out[j])
            h = expert_choice_ffn(h, ffn_norm[i], router_w[i].T, expert_w_gate[i].astype(BF16),
                                  expert_w_up[i].astype(BF16), expert_w_down[i].astype(BF16))
        outs.append(_final_norm_call(h, final_norm))
    return jnp.stack(outs)
```

```python
import functools
import math

import jax
import jax.numpy as jnp
from jax import lax
from jax.experimental import pallas as pl
from jax.experimental.pallas import tpu as pltpu

F32 = jnp.float32
BF16 = jnp.bfloat16
I32 = jnp.int32

EPS = 1e-6
LANES = 128
VMEM_LIMIT_BYTES = 56 * 1024 * 1024
GLA_HEADS = 8
GLA_GATE_RANK = 16
GLA_GATE_TAU = 16.0
GLA_CHUNK = 64
MLA_NOPE = 128
MLA_ROPE = 64
MLA_V = 128
MLA_QK_PAD = 256
ROPE_THETA = 10000.0
N_EXPERTS = 16
CAPACITY_FACTOR = 2

NT_DIMS = (((1,), (1,)), ((), ()))
TN_DIMS = (((0,), (0,)), ((), ()))


def _cparams(*sem):
    return pltpu.CompilerParams(dimension_semantics=sem, vmem_limit_bytes=VMEM_LIMIT_BYTES)


def _tile(n, pref):
    t = min(n, pref)
    assert n % t == 0, (n, pref)
    return t


def _dot(a, b):
    return jnp.dot(a, b, preferred_element_type=F32)


def _split3(a):
    p0 = a.astype(BF16)
    r0 = a - p0.astype(F32)
    p1 = r0.astype(BF16)
    p2 = (r0 - p1.astype(F32)).astype(BF16)
    return p0, p1, p2


def _dot_exact_lhs(a, b01):
    p0, p1, p2 = _split3(a)
    return (_dot(p0, b01) + _dot(p1, b01)) + _dot(p2, b01)


def _rms(x, gain):
    return x * lax.rsqrt(jnp.mean(x * x, axis=-1, keepdims=True) + EPS) * gain


def _norm_mm_kernel(h_ref, g_ref, w_ref, o_ref, xn_ref):
    @pl.when(pl.program_id(1) == 0)
    def _():
        xn_ref[...] = _rms(h_ref[...], g_ref[...]).astype(BF16)

    o_ref[...] = _dot(xn_ref[...], w_ref[...]).astype(o_ref.dtype)


def norm_matmul(h, gain, w, out_dtype, tm=512, tn=512):
    s, d = h.shape
    n = w.shape[1]
    tm, tn = _tile(s, tm), _tile(n, tn)
    return pl.pallas_call(
        _norm_mm_kernel,
        grid=(s // tm, n // tn),
        in_specs=[pl.BlockSpec((tm, d), lambda i, j: (i, 0)),
                  pl.BlockSpec((1, d), lambda i, j: (0, 0)),
                  pl.BlockSpec((d, tn), lambda i, j: (0, j))],
        out_specs=pl.BlockSpec((tm, tn), lambda i, j: (i, j)),
        out_shape=jax.ShapeDtypeStruct((s, n), out_dtype),
        scratch_shapes=[pltpu.VMEM((tm, d), BF16)],
        compiler_params=_cparams("parallel", "arbitrary"),
        name="norm_matmul",
    )(h, gain.reshape(1, d), w)


def _mm_res_kernel(*refs, n_in):
    a_refs, w_refs = refs[:n_in], refs[n_in:2 * n_in]
    res_ref, o_ref = refs[2 * n_in], refs[2 * n_in + 1]
    acc = res_ref[...]
    for a, w in zip(a_refs, w_refs):
        acc = acc + _dot(a[...], w[...])
    o_ref[...] = acc


def matmul_residual(a_list, w_list, res, tm=512, tn=512):
    s, n = res.shape
    tm, tn = _tile(s, tm), _tile(n, tn)
    in_specs = [pl.BlockSpec((tm, a.shape[1]), lambda i, j: (i, 0)) for a in a_list]
    in_specs += [pl.BlockSpec((w.shape[0], tn), lambda i, j: (0, j)) for w in w_list]
    in_specs += [pl.BlockSpec((tm, tn), lambda i, j: (i, j))]
    return pl.pallas_call(
        functools.partial(_mm_res_kernel, n_in=len(a_list)),
        grid=(s // tm, n // tn),
        in_specs=in_specs,
        out_specs=pl.BlockSpec((tm, tn), lambda i, j: (i, j)),
        out_shape=jax.ShapeDtypeStruct((s, n), F32),
        compiler_params=_cparams("parallel", "arbitrary"),
        name="matmul_residual",
    )(*a_list, *w_list, res)


HALO = 16


def _conv_kernel(b_ref, c_ref, u_ref, cp_ref, up_ref, cn_ref, un_ref, w_ref, o_ref):
    i, nt = pl.program_id(0), pl.num_programs(0)
    tm = c_ref.shape[0]
    z = c_ref[...].astype(F32) * u_ref[...].astype(F32)
    zp = cp_ref[HALO - 1:HALO, :].astype(F32) * up_ref[HALO - 1:HALO, :].astype(F32)
    zn = cn_ref[0:1, :].astype(F32) * un_ref[0:1, :].astype(F32)
    zp = jnp.where(i > 0, zp, 0.0)
    zn = jnp.where(i < nt - 1, zn, 0.0)
    rows = lax.broadcasted_iota(I32, z.shape, 0)
    z_prev = jnp.where(rows == 0, zp, pltpu.roll(z, 1, axis=0))
    z_next = jnp.where(rows == tm - 1, zn, pltpu.roll(z, tm - 1, axis=0))
    w = w_ref[...]
    y = w[0:1] * z_prev + w[1:2] * z + w[2:3] * z_next
    o_ref[...] = (b_ref[...].astype(F32) * y).astype(o_ref.dtype)


def gated_conv(proj, conv_w, width, tm=512, tc=512):
    s = proj.shape[0]
    tm, tc = _tile(s, tm), _tile(width, tc)
    nc = width // tc
    nh = s // HALO
    per = tm // HALO

    def prev_map(off):
        return lambda i, c: (jnp.maximum(i * per - 1, 0), off * nc + c)

    def next_map(off):
        return lambda i, c: (jnp.minimum((i + 1) * per, nh - 1), off * nc + c)

    return pl.pallas_call(
        _conv_kernel,
        grid=(s // tm, nc),
        in_specs=[pl.BlockSpec((tm, tc), lambda i, c: (i, c)),
                  pl.BlockSpec((tm, tc), lambda i, c: (i, nc + c)),
                  pl.BlockSpec((tm, tc), lambda i, c: (i, 2 * nc + c)),
                  pl.BlockSpec((HALO, tc), prev_map(1)),
                  pl.BlockSpec((HALO, tc), prev_map(2)),
                  pl.BlockSpec((HALO, tc), next_map(1)),
                  pl.BlockSpec((HALO, tc), next_map(2)),
                  pl.BlockSpec((3, tc), lambda i, c: (0, c))],
        out_specs=pl.BlockSpec((tm, tc), lambda i, c: (i, c)),
        out_shape=jax.ShapeDtypeStruct((s, width), BF16),
        compiler_params=_cparams("parallel", "parallel"),
        name="gated_conv",
    )(proj, proj, proj, proj, proj, proj, proj, conv_w)


def _log_sigmoid(z):
    return -(jnp.maximum(-z, 0.0) + jnp.log1p(jnp.exp(-jnp.abs(z))))


def _gla_log_decay(a_ref, wa_ref, ba_ref):
    a = a_ref[...]
    w = wa_ref[...]
    a_hi = a.astype(BF16)
    a_lo = (a - a_hi.astype(F32)).astype(BF16)
    w_hi = w.astype(BF16)
    w_lo = (w - w_hi.astype(F32)).astype(BF16)
    z = (_dot(a_hi, w_hi) + _dot(a_lo, w_hi)) + _dot(a_hi, w_lo) + ba_ref[...]
    return _log_sigmoid(z) / GLA_GATE_TAU


def _gla_chunk(q, k, v, la, state_t, reverse):
    n = q.shape[0]
    r = lax.broadcasted_iota(I32, (n, n), 0)
    c = lax.broadcasted_iota(I32, (n, n), 1)
    tri = jnp.where((c >= r) if reverse else (c <= r), 1.0, 0.0).astype(BF16)
    la_hi = la.astype(BF16)
    la_lo = (la - la_hi.astype(F32)).astype(BF16)
    cum = _dot(tri, la_hi) + _dot(tri, la_lo)
    cum_last = cum[0:1] if reverse else cum[n - 1:n]
    qd = (q * jnp.exp(cum)).astype(BF16)
    kd = (k * jnp.exp(-cum)).astype(BF16)
    ke = (k * jnp.exp(cum_last - cum)).astype(BF16)
    dec = jnp.exp(cum_last)
    sc = lax.dot_general(qd, kd, NT_DIMS, preferred_element_type=F32)
    sc = jnp.where((c > r) if reverse else (c <= r), sc, 0.0).astype(BF16)
    o = _dot(sc, v) + lax.dot_general(qd, state_t.astype(BF16), NT_DIMS, preferred_element_type=F32)
    new_state = state_t * dec + lax.dot_general(v, ke, TN_DIMS, preferred_element_type=F32)
    return o, new_state


def _gla_scan_tile(q_ref, k_ref, v_ref, la, st_ref, reverse, emit):
    tc = q_ref.shape[0]
    nchunk = tc // GLA_CHUNK
    scale = q_ref.shape[1] ** -0.5
    order = range(nchunk - 1, -1, -1) if reverse else range(nchunk)
    state = st_ref[...]
    for ci in order:
        sl = slice(ci * GLA_CHUNK, (ci + 1) * GLA_CHUNK)
        q = q_ref[sl, :].astype(F32) * scale
        k = k_ref[sl, :].astype(F32)
        o, state = _gla_chunk(q, k, v_ref[sl, :], la[sl, :], state, reverse)
        emit(sl, o)
    st_ref[...] = state


def _gla_fwd_kernel(q_ref, k_ref, v_ref, a_ref, wa_ref, ba_ref, o_ref, st_ref):
    @pl.when(pl.program_id(1) == 0)
    def _():
        st_ref[...] = jnp.zeros_like(st_ref)

    la = _gla_log_decay(a_ref, wa_ref, ba_ref)

    def emit(sl, o):
        o_ref[sl, :] = o

    _gla_scan_tile(q_ref, k_ref, v_ref, la, st_ref, False, emit)


def _gla_bwd_kernel(q_ref, k_ref, v_ref, a_ref, wa_ref, ba_ref, of_ref, g_ref, hn_ref, y_ref, st_ref):
    @pl.when(pl.program_id(1) == 0)
    def _():
        st_ref[...] = jnp.zeros_like(st_ref)

    la = _gla_log_decay(a_ref, wa_ref, ba_ref)

    def emit(sl, o):
        tot = _rms(of_ref[sl, :] + o, hn_ref[...])
        g = g_ref[sl, :].astype(F32)
        y_ref[sl, :] = (tot * (g * jax.nn.sigmoid(g))).astype(y_ref.dtype)

    _gla_scan_tile(q_ref, k_ref, v_ref, la, st_ref, True, emit)


def gla_mixer(proj, a_lr, wa_f, ba_f, wa_b, ba_b, head_norm, col_q, col_k, col_v, col_g, dk, dv, tc=512):
    s = proj.shape[0]
    tc = _tile(s, tc)
    nt = s // tc
    h_ = GLA_HEADS
    lr = a_lr.shape[1]

    def specs(row):
        return [pl.BlockSpec((tc, dk), lambda h, i: (row(i), col_q // dk + h)),
                pl.BlockSpec((tc, dk), lambda h, i: (row(i), col_k // dk + h)),
                pl.BlockSpec((tc, dv), lambda h, i: (row(i), col_v // dv + h)),
                pl.BlockSpec((tc, lr), lambda h, i: (row(i), 0)),
                pl.BlockSpec((lr, dk), lambda h, i: (0, h)),
                pl.BlockSpec((1, dk), lambda h, i: (0, h))]

    fwd_row = lambda i: i
    o_fwd = pl.pallas_call(
        _gla_fwd_kernel,
        grid=(h_, nt),
        in_specs=specs(fwd_row),
        out_specs=pl.BlockSpec((tc, dv), lambda h, i: (i, h)),
        out_shape=jax.ShapeDtypeStruct((s, h_ * dv), F32),
        scratch_shapes=[pltpu.VMEM((dv, dk), F32)],
        compiler_params=_cparams("parallel", "arbitrary"),
        name="gla_fwd",
    )(proj, proj, proj, a_lr, wa_f, ba_f)

    bwd_row = lambda i: nt - 1 - i
    return pl.pallas_call(
        _gla_bwd_kernel,
        grid=(h_, nt),
        in_specs=specs(bwd_row) + [
            pl.BlockSpec((tc, dv), lambda h, i: (bwd_row(i), h)),
            pl.BlockSpec((tc, dv), lambda h, i: (bwd_row(i), col_g // dv + h)),
            pl.BlockSpec((1, dv), lambda h, i: (0, 0))],
        out_specs=pl.BlockSpec((tc, dv), lambda h, i: (bwd_row(i), h)),
        out_shape=jax.ShapeDtypeStruct((s, h_ * dv), BF16),
        scratch_shapes=[pltpu.VMEM((dv, dk), F32)],
        compiler_params=_cparams("parallel", "arbitrary"),
        name="gla_bwd",
    )(proj, proj, proj, a_lr, wa_b, ba_b, o_fwd, proj, head_norm)


def _rope_tables_kernel(pos_ref, invf_ref, cos_ref, sina_ref, sinb_ref, *, axis):
    ang = pos_ref[...].astype(F32) * invf_ref[...]
    feat = lax.broadcasted_iota(I32, ang.shape, axis)
    half = MLA_ROPE // 2
    c, s_ = jnp.cos(ang), jnp.sin(ang)
    cos_ref[...] = jnp.where(feat < 2 * half, c, 0.0)
    sina_ref[...] = jnp.where(feat < half, -s_, 0.0)
    sinb_ref[...] = jnp.where((feat >= half) & (feat < 2 * half), s_, 0.0)


def rope_tables(positions, axis, tm=512):
    s = positions.shape[0]
    tm = _tile(s, tm)
    half = MLA_ROPE // 2
    inv_freq = ROPE_THETA ** (-jnp.arange(0, MLA_ROPE, 2, dtype=F32) / MLA_ROPE)
    invf = jnp.concatenate([inv_freq, inv_freq, jnp.zeros((LANES - 2 * half,), F32)])
    if axis == 1:
        pos, invf = positions.reshape(s, 1), invf.reshape(1, LANES)
        pos_spec, invf_spec = pl.BlockSpec((tm, 1), lambda i: (i, 0)), pl.BlockSpec((1, LANES), lambda i: (0, 0))
        spec, shape = pl.BlockSpec((tm, LANES), lambda i: (i, 0)), (s, LANES)
    else:
        pos, invf = positions.reshape(1, s), invf.reshape(LANES, 1)
        pos_spec, invf_spec = pl.BlockSpec((1, tm), lambda i: (0, i)), pl.BlockSpec((LANES, 1), lambda i: (0, 0))
        spec, shape = pl.BlockSpec((LANES, tm), lambda i: (0, i)), (LANES, s)
    return pl.pallas_call(
        functools.partial(_rope_tables_kernel, axis=axis),
        grid=(s // tm,),
        in_specs=[pos_spec, invf_spec],
        out_specs=[spec, spec, spec],
        out_shape=[jax.ShapeDtypeStruct(shape, F32)] * 3,
        compiler_params=_cparams("parallel"),
        name="rope_tables",
    )(pos, invf)


def _rope(x, cos_t, sin_a, sin_b, axis):
    half = MLA_ROPE // 2
    return x * cos_t + pltpu.roll(x, LANES - half, axis=axis) * sin_a + pltpu.roll(x, half, axis=axis) * sin_b


def _mla_post_kernel(d_ref, qn_ref, kvn_ref, cos_ref, sina_ref, sinb_ref, cq_ref, ckv_ref, kr_ref, *, q_lora, kv_lora):
    d = d_ref[...]
    cq_ref[...] = _rms(d[:, :q_lora], qn_ref[...]).astype(BF16)
    ckv_ref[...] = _rms(d[:, q_lora:q_lora + kv_lora], kvn_ref[...]).astype(BF16)
    kr = d[:, q_lora + kv_lora:]
    kr_ref[...] = _rope(kr, cos_ref[...], sina_ref[...], sinb_ref[...], 1).astype(BF16)


def mla_post(down, q_norm, kv_norm, tables, q_lora, kv_lora, tm=512):
    s, n = down.shape
    tm = _tile(s, tm)
    tab = pl.BlockSpec((tm, LANES), lambda i: (i, 0))
    return pl.pallas_call(
        functools.partial(_mla_post_kernel, q_lora=q_lora, kv_lora=kv_lora),
        grid=(s // tm,),
        in_specs=[pl.BlockSpec((tm, n), lambda i: (i, 0)),
                  pl.BlockSpec((1, q_lora), lambda i: (0, 0)),
                  pl.BlockSpec((1, kv_lora), lambda i: (0, 0)), tab, tab, tab],
        out_specs=[pl.BlockSpec((tm, q_lora), lambda i: (i, 0)),
                   pl.BlockSpec((tm, kv_lora), lambda i: (i, 0)),
                   pl.BlockSpec((tm, LANES), lambda i: (i, 0))],
        out_shape=[jax.ShapeDtypeStruct((s, q_lora), BF16),
                   jax.ShapeDtypeStruct((s, kv_lora), BF16),
                   jax.ShapeDtypeStruct((s, LANES), BF16)],
        compiler_params=_cparams("parallel"),
        name="mla_post",
    )(down, q_norm.reshape(1, -1), kv_norm.reshape(1, -1), *tables)


HEAD_GROUP = 4


def _q_up_kernel(x_ref, w_ref, cos_ref, sina_ref, sinb_ref, q_ref, *, qscale):
    x = x_ref[...]
    for g in range(w_ref.shape[0]):
        a = lax.dot_general(w_ref[g], x, NT_DIMS, preferred_element_type=F32)
        rot = _rope(a[MLA_NOPE:, :], cos_ref[...], sina_ref[...], sinb_ref[...], 0)
        q_ref[g] = (jnp.concatenate([a[:MLA_NOPE, :], rot], axis=0) * qscale).astype(BF16)


def mla_q_up(cq, w_uq_t, tables_t, qscale, tm=512):
    s, r = cq.shape
    h_ = w_uq_t.shape[0]
    tm = _tile(s, tm)
    g = _tile(h_, HEAD_GROUP)
    tab = pl.BlockSpec((LANES, tm), lambda i, h: (0, i))
    return pl.pallas_call(
        functools.partial(_q_up_kernel, qscale=qscale),
        grid=(s // tm, h_ // g),
        in_specs=[pl.BlockSpec((tm, r), lambda i, h: (i, 0)),
                  pl.BlockSpec((g, MLA_QK_PAD, r), lambda i, h: (h, 0, 0)), tab, tab, tab],
        out_specs=pl.BlockSpec((g, MLA_QK_PAD, tm), lambda i, h: (h, 0, i)),
        out_shape=jax.ShapeDtypeStruct((h_, MLA_QK_PAD, s), BF16),
        compiler_params=_cparams("parallel", "arbitrary"),
        name="mla_q_up",
    )(cq, w_uq_t, *tables_t)


def _kv_up_kernel(x_ref, wk_ref, wvt_ref, kr_ref, k_ref, vt_ref):
    x = x_ref[...]
    for g in range(wk_ref.shape[0]):
        k_ref[g] = jnp.concatenate([_dot(x, wk_ref[g]).astype(BF16), kr_ref[...]], axis=1)
        vt_ref[g] = lax.dot_general(wvt_ref[g], x, NT_DIMS, preferred_element_type=F32).astype(BF16)


def mla_kv_up(ckv, w_uk_h, w_uv_t, kr, tm=512):
    s, r = ckv.shape
    h_ = w_uk_h.shape[0]
    tm = _tile(s, tm)
    g = _tile(h_, HEAD_GROUP)
    return pl.pallas_call(
        _kv_up_kernel,
        grid=(s // tm, h_ // g),
        in_specs=[pl.BlockSpec((tm, r), lambda i, h: (i, 0)),
                  pl.BlockSpec((g, r, MLA_NOPE), lambda i, h: (h, 0, 0)),
                  pl.BlockSpec((g, MLA_V, r), lambda i, h: (h, 0, 0)),
                  pl.BlockSpec((tm, LANES), lambda i, h: (i, 0))],
        out_specs=[pl.BlockSpec((g, tm, MLA_QK_PAD), lambda i, h: (h, i, 0)),
                   pl.BlockSpec((g, MLA_V, tm), lambda i, h: (h, 0, i))],
        out_shape=[jax.ShapeDtypeStruct((h_, s, MLA_QK_PAD), BF16),
                   jax.ShapeDtypeStruct((h_, MLA_V, s), BF16)],
        compiler_params=_cparams("parallel", "arbitrary"),
        name="mla_kv_up",
    )(ckv, w_uk_h, w_uv_t, kr)


def _attn_kernel(qt_ref, k_ref, vt_ref, o_ref, s_ref, acc_ref, *, tkb):
    s_len = k_ref.shape[1]
    nkb = s_len // tkb
    tq = qt_ref.shape[2]
    qt = qt_ref[0]

    def scores(j, slot):
        off = pl.multiple_of(j * tkb, tkb)
        s_ref[slot] = _dot(k_ref[0, pl.ds(off, tkb), :], qt)

    def softmax_pv(j, slot, m, l):
        s = s_ref[slot]
        m_new = jnp.maximum(m, jnp.max(s, axis=0, keepdims=True))
        alpha = jnp.exp2(m - m_new)
        p = jnp.exp2(s - m_new)
        l = alpha * l + jnp.sum(p, axis=0, keepdims=True)
        off = pl.multiple_of(j * tkb, tkb)
        acc_ref[...] = alpha * acc_ref[...] + _dot(vt_ref[0, :, pl.ds(off, tkb)], p.astype(BF16))
        return m_new, l

    acc_ref[...] = jnp.zeros_like(acc_ref)
    m = jnp.full((1, tq), -jnp.inf, F32)
    l = jnp.zeros((1, tq), F32)
    scores(0, 0)
    if nkb > 1:
        def pair(i, carry):
            m, l = carry
            scores(2 * i + 1, 1)
            m, l = softmax_pv(2 * i, 0, m, l)
            scores(2 * i + 2, 0)
            return softmax_pv(2 * i + 1, 1, m, l)

        m, l = lax.fori_loop(0, nkb // 2 - 1, pair, (m, l))
        scores(nkb - 1, 1)
        m, l = softmax_pv(nkb - 2, 0, m, l)
        m, l = softmax_pv(nkb - 1, 1, m, l)
    else:
        m, l = softmax_pv(0, 0, m, l)
    o_ref[...] = (acc_ref[...] / l).T.astype(o_ref.dtype)


def mla_attention(qt, k, vt, tq=512, tkb=512):
    h_, dq, s = qt.shape
    dv = vt.shape[1]
    tq, tkb = _tile(s, tq), _tile(s, tkb)
    assert (s // tkb) % 2 == 0 or s == tkb
    return pl.pallas_call(
        functools.partial(_attn_kernel, tkb=tkb),
        grid=(h_, s // tq),
        in_specs=[pl.BlockSpec((1, dq, tq), lambda h, i: (h, 0, i)),
                  pl.BlockSpec((1, s, dq), lambda h, i: (h, 0, 0)),
                  pl.BlockSpec((1, dv, s), lambda h, i: (h, 0, 0))],
        out_specs=pl.BlockSpec((tq, dv), lambda h, i: (i, h)),
        out_shape=jax.ShapeDtypeStruct((s, h_ * dv), BF16),
        scratch_shapes=[pltpu.VMEM((2, tkb, tq), F32), pltpu.VMEM((dv, tq), F32)],
        compiler_params=_cparams("parallel", "arbitrary"),
        name="mla_attention",
    )(qt, k, vt)


def _router_kernel(h_ref, g_ref, wt_ref, aff_ref):
    xn = _rms(h_ref[...], g_ref[...])
    x0, x1, x2 = _split3(xn)
    w0, w1, w2 = _split3(wt_ref[...])

    def nt(a, b):
        return lax.dot_general(a, b, NT_DIMS, preferred_element_type=F32)

    logits = (((nt(w2, x0) + nt(w0, x2)) + nt(w1, x1)) + (nt(w1, x0) + nt(w0, x1))) + nt(w0, x0)
    e = jnp.exp(logits - jnp.max(logits, axis=0, keepdims=True))
    aff_ref[...] = e / jnp.sum(e, axis=0, keepdims=True)


def router_affinity(h, gain, w_router_t, tm=512):
    s, d = h.shape
    e_ = w_router_t.shape[0]
    tm = _tile(s, tm)
    return pl.pallas_call(
        _router_kernel,
        grid=(s // tm,),
        in_specs=[pl.BlockSpec((tm, d), lambda i: (i, 0)),
                  pl.BlockSpec((1, d), lambda i: (0, 0)),
                  pl.BlockSpec((e_, d), lambda i: (0, 0))],
        out_specs=pl.BlockSpec((e_, tm), lambda i: (0, i)),
        out_shape=jax.ShapeDtypeStruct((e_, s), F32),
        compiler_params=_cparams("parallel"),
        name="router_affinity",
    )(h, gain.reshape(1, d), w_router_t)


def _prefix_parts(xb, u_incl, ones_b, sl_strict):
    loc = _dot(xb, u_incl)
    tot = _dot(xb, ones_b)
    base = _dot(sl_strict, tot.astype(BF16))
    return loc, tot, base


def _select_kernel(aff_ref, idx_ref, gate_ref, rank_ref, start_ref, end_ref, sel_ref, *, cap):
    e_, nt, ln = aff_ref.shape
    li = lax.broadcasted_iota(I32, (ln, ln), 0)
    lj = lax.broadcasted_iota(I32, (ln, ln), 1)
    u_incl = jnp.where(li <= lj, 1.0, 0.0).astype(BF16)
    u_incl_t = jnp.where(lj <= li, 1.0, 0.0).astype(BF16)
    ones_b = jnp.ones((ln, ln), BF16)
    ti = lax.broadcasted_iota(I32, (nt, nt), 0)
    tj = lax.broadcasted_iota(I32, (nt, nt), 1)
    sl_strict = jnp.where(tj < ti, 1.0, 0.0).astype(BF16)

    bits = pltpu.bitcast(aff_ref[...], I32)

    def count_ge(thr):
        ge = jnp.where(bits >= thr, 1.0, 0.0)
        return jnp.sum(jnp.sum(ge, axis=2, keepdims=True), axis=1, keepdims=True)

    def search(b, thr):
        cand = thr | jnp.left_shift(jnp.int32(1), 30 - b)
        return jnp.where(count_ge(cand) >= cap, cand, thr)

    thr = lax.fori_loop(0, 31, search, jnp.zeros((e_, 1, 1), I32))
    gt = jnp.where(bits > thr, 1.0, 0.0)
    eq = jnp.where(bits == thr, 1.0, 0.0)
    need = cap - jnp.sum(jnp.sum(gt, axis=2, keepdims=True), axis=1, keepdims=True)

    cnt = jnp.zeros((nt, ln), F32)
    for e in range(e_):
        loc, _, base = _prefix_parts(eq[e].astype(BF16), u_incl, ones_b, sl_strict)
        eq_rank = loc + base - eq[e]
        sel = gt[e] + eq[e] * jnp.where(eq_rank < need[e], 1.0, 0.0)
        sel_ref[e] = sel
        cnt = cnt + sel

    cnt_b = cnt.astype(BF16)
    loc_c = _dot(cnt_b, u_incl)
    tot_c = _dot(cnt_b, ones_b)
    t0, t1, t2 = _split3(tot_c)
    base_c = (_dot(sl_strict, t0) + _dot(sl_strict, t1)) + _dot(sl_strict, t2)
    end = loc_c + base_c
    start = end - cnt
    start_ref[...] = start.astype(I32)
    end_ref[...] = end.astype(I32)

    p_row = lax.broadcasted_iota(I32, (1, cap), 1).astype(F32)
    tile_iota = lax.broadcasted_iota(I32, (nt, cap), 0).astype(F32)
    lane_iota = lax.broadcasted_iota(I32, (ln, cap), 0).astype(F32)

    def compact(e, within):
        sel = sel_ref[e]
        selb = sel.astype(BF16)
        _, tot, base = _prefix_parts(selb, u_incl, ones_b, sl_strict)
        tile_end = (base + tot)[:, 0:1]
        jp = jnp.sum(jnp.where(tile_end <= p_row, 1.0, 0.0), axis=0, keepdims=True)
        g_t = jnp.where(tile_iota == jp, 1.0, 0.0)
        base_p = jnp.sum(g_t * base[:, 0:1], axis=0, keepdims=True)
        r_p = p_row - base_p
        loc_t = lax.dot_general(u_incl_t, selb, NT_DIMS, preferred_element_type=F32)
        sel_loc = _dot(loc_t.astype(BF16), g_t.astype(BF16))
        lane_p = jnp.sum(jnp.where(sel_loc <= r_p, 1.0, 0.0), axis=0, keepdims=True)
        l_t = jnp.where(lane_iota == lane_p, 1.0, 0.0)
        l_b = l_t.astype(BF16)
        idx_ref[e] = (jp * ln + lane_p).astype(I32)
        gate_ref[e] = jnp.sum(g_t * _dot_exact_lhs(aff_ref[e], l_b), axis=0, keepdims=True)
        rank_full = start + within
        rank_ref[e] = jnp.sum(g_t * _dot_exact_lhs(rank_full, l_b), axis=0, keepdims=True).astype(I32)
        return within + sel

    lax.fori_loop(0, e_, compact, jnp.zeros((nt, ln), F32))


def expert_select(aff3, cap):
    e_, nt, ln = aff3.shape
    full3 = lambda a, b, c: pl.BlockSpec((a, b, c), lambda i: (0, 0, 0))
    full2 = lambda a, b: pl.BlockSpec((a, b), lambda i: (0, 0))
    return pl.pallas_call(
        functools.partial(_select_kernel, cap=cap),
        grid=(1,),
        in_specs=[full3(e_, nt, ln)],
        out_specs=[full3(e_, 1, cap), full3(e_, 1, cap), full3(e_, 1, cap), full2(nt, ln), full2(nt, ln)],
        out_shape=[jax.ShapeDtypeStruct((e_, 1, cap), I32),
                   jax.ShapeDtypeStruct((e_, 1, cap), F32),
                   jax.ShapeDtypeStruct((e_, 1, cap), I32),
                   jax.ShapeDtypeStruct((nt, ln), I32),
                   jax.ShapeDtypeStruct((nt, ln), I32)],
        scratch_shapes=[pltpu.VMEM((e_, nt, ln), F32)],
        compiler_params=_cparams("arbitrary"),
        name="expert_select",
    )(aff3)


DMA_ISSUE_UNROLL = 8


def _row_gather_copy(h_hbm, xbuf, sem, slot, tok, p):
    return pltpu.make_async_copy(h_hbm.at[pl.ds(tok, 1)], xbuf.at[slot, pl.ds(p, 1)], sem.at[slot])


def _ffn_up_kernel(idx_ref, h_hbm, g_ref, wg_ref, wu_ref, hid_ref, xbuf, sem):
    tm = xbuf.shape[1]
    step = pl.program_id(0) * pl.num_programs(1) + pl.program_id(1)
    nsteps = pl.num_programs(0) * pl.num_programs(1)
    slot = step % 2

    def issue(s_, slot_):
        def body(p, carry):
            _row_gather_copy(h_hbm, xbuf, sem, slot_, idx_ref[s_ * tm + p], p).start()
            return carry
        lax.fori_loop(0, tm, body, 0, unroll=DMA_ISSUE_UNROLL)

    @pl.when(step == 0)
    def _():
        issue(step, slot)

    @pl.when(step + 1 < nsteps)
    def _():
        issue(step + 1, 1 - slot)

    pltpu.make_async_copy(h_hbm.at[pl.ds(0, tm)], xbuf.at[slot], sem.at[slot]).wait()
    xn = _rms(xbuf[slot], g_ref[...]).astype(BF16)
    gate = _dot(xn, wg_ref[0])
    up = _dot(xn, wu_ref[0])
    hid_ref[...] = (gate * jax.nn.sigmoid(gate) * up).astype(hid_ref.dtype)


def ffn_up(h, gain, idx_flat, w_gate, w_up, cap, tm=512):
    s, d = h.shape
    e_, _, ff = w_gate.shape
    tm = _tile(cap, tm)
    r = cap // tm
    grid_spec = pltpu.PrefetchScalarGridSpec(
        num_scalar_prefetch=1,
        grid=(e_, r),
        in_specs=[pl.BlockSpec(memory_space=pl.ANY),
                  pl.BlockSpec((1, d), lambda e, i, idx: (0, 0)),
                  pl.BlockSpec((1, d, ff), lambda e, i, idx: (e, 0, 0)),
                  pl.BlockSpec((1, d, ff), lambda e, i, idx: (e, 0, 0))],
        out_specs=pl.BlockSpec((tm, ff), lambda e, i, idx: (e * r + i, 0)),
        scratch_shapes=[pltpu.VMEM((2, tm, d), F32), pltpu.SemaphoreType.DMA((2,))],
    )
    return pl.pallas_call(
        _ffn_up_kernel,
        grid_spec=grid_spec,
        out_shape=jax.ShapeDtypeStruct((e_ * cap, ff), BF16),
        compiler_params=_cparams("arbitrary", "arbitrary"),
        name="ffn_up",
    )(idx_flat, h, gain.reshape(1, d), w_gate, w_up)


def _row_scatter_copy(ybuf, y_hbm, sem, slot, p, row):
    return pltpu.make_async_copy(ybuf.at[slot, pl.ds(p, 1)], y_hbm.at[pl.ds(row, 1)], sem.at[slot])


def _ffn_down_kernel(rank_ref, hid_ref, wd_ref, gate_ref, y_hbm, ybuf, sem):
    tm = ybuf.shape[1]
    step = pl.program_id(0) * pl.num_programs(1) + pl.program_id(1)
    nsteps = pl.num_programs(0) * pl.num_programs(1)
    slot = step % 2

    def drain(slot_):
        pltpu.make_async_copy(ybuf.at[slot_], y_hbm.at[pl.ds(0, tm)], sem.at[slot_]).wait()

    @pl.when(step >= 2)
    def _():
        drain(slot)

    ybuf[slot] = _dot(hid_ref[...], wd_ref[0]) * gate_ref[...]

    def body(p, carry):
        _row_scatter_copy(ybuf, y_hbm, sem, slot, p, rank_ref[step * tm + p]).start()
        return carry
    lax.fori_loop(0, tm, body, 0, unroll=DMA_ISSUE_UNROLL)

    @pl.when(step == nsteps - 1)
    def _():
        drain(slot)

        @pl.when(nsteps >= 2)
        def _():
            drain(1 - slot)


def ffn_down(hid, w_down, gate_col, rank_flat, cap, tm=512):
    n, ff = hid.shape
    e_, _, d = w_down.shape
    tm = _tile(cap, tm)
    r = cap // tm
    grid_spec = pltpu.PrefetchScalarGridSpec(
        num_scalar_prefetch=1,
        grid=(e_, r),
        in_specs=[pl.BlockSpec((tm, ff), lambda e, i, rk: (e * r + i, 0)),
                  pl.BlockSpec((1, ff, d), lambda e, i, rk: (e, 0, 0)),
                  pl.BlockSpec((tm, 1), lambda e, i, rk: (e * r + i, 0))],
        out_specs=pl.BlockSpec(memory_space=pl.ANY),
        scratch_shapes=[pltpu.VMEM((2, tm, d), F32), pltpu.SemaphoreType.DMA((2,))],
    )
    return pl.pallas_call(
        _ffn_down_kernel,
        grid_spec=grid_spec,
        out_shape=jax.ShapeDtypeStruct((n, d), F32),
        compiler_params=_cparams("arbitrary", "arbitrary"),
        name="ffn_down",
    )(rank_flat, hid, w_down, gate_col)


def _combine_kernel(ws_ref, h_ref, st_ref, en_ref, y_hbm, o_ref, win, sem, *, total):
    i = pl.program_id(0)
    wrows = win.shape[0]
    lo, hi = ws_ref[i], ws_ref[i + 1]
    first = (lo // 8) * 8
    nwin = (hi - first + wrows - 1) // wrows
    st, en = st_ref[...], en_ref[...]

    def body(k, acc):
        w = first + k * wrows
        w0 = pl.multiple_of(jnp.minimum(w, total - wrows), 8)
        cp = pltpu.make_async_copy(y_hbm.at[pl.ds(w0, wrows)], win, sem)
        cp.start()
        cp.wait()
        r_abs = w0 + lax.broadcasted_iota(I32, (1, wrows), 1)
        q = jnp.where((st <= r_abs) & (r_abs < en) & (r_abs >= w), 1.0, 0.0).astype(BF16)
        return acc + _dot(q, win[...].astype(BF16))

    acc = lax.fori_loop(0, nwin, body, jnp.zeros(o_ref.shape, F32))
    o_ref[...] = h_ref[...] + acc


def ffn_combine(h, y_sorted, start_col, end_col, win_starts, tm=256, wrows=256):
    s, d = h.shape
    total = y_sorted.shape[0]
    tm = _tile(s, tm)
    wrows = _tile(total, wrows)
    grid_spec = pltpu.PrefetchScalarGridSpec(
        num_scalar_prefetch=1,
        grid=(s // tm,),
        in_specs=[pl.BlockSpec((tm, d), lambda i, ws: (i, 0)),
                  pl.BlockSpec((tm, 1), lambda i, ws: (i, 0)),
                  pl.BlockSpec((tm, 1), lambda i, ws: (i, 0)),
                  pl.BlockSpec(memory_space=pl.ANY)],
        out_specs=pl.BlockSpec((tm, d), lambda i, ws: (i, 0)),
        scratch_shapes=[pltpu.VMEM((wrows, d), F32), pltpu.SemaphoreType.DMA(())],
    )
    return pl.pallas_call(
        functools.partial(_combine_kernel, total=total),
        grid_spec=grid_spec,
        out_shape=jax.ShapeDtypeStruct((s, d), F32),
        compiler_params=_cparams("arbitrary"),
        name="ffn_combine",
    )(win_starts, h, start_col, end_col, y_sorted)


def expert_choice_ffn(h, gain, w_router_t, w_gate, w_up, w_down, combine_tm=256):
    s, d = h.shape
    e_ = w_router_t.shape[0]
    cap = CAPACITY_FACTOR * s // e_
    aff = router_affinity(h, gain, w_router_t)
    idx, gate, rank, start, end = expert_select(aff.reshape(e_, s // LANES, LANES), cap)
    hid = ffn_up(h, gain, idx.reshape(-1), w_gate, w_up, cap)
    y_sorted = ffn_down(hid, w_down, gate.reshape(-1, 1), rank.reshape(-1), cap)
    start_flat = start.reshape(-1)
    tm = _tile(s, combine_tm)
    win_starts = jnp.concatenate([start_flat[::tm], jnp.full((1,), e_ * cap, I32)])
    return ffn_combine(h, y_sorted, start_flat.reshape(s, 1), end.reshape(s, 1), win_starts, tm=tm)


def _final_norm_kernel(h_ref, g_ref, o_ref):
    o_ref[...] = _rms(h_ref[...], g_ref[...])


def _final_norm_call(h, gain, tm=512):
    s, d = h.shape
    tm = _tile(s, tm)
    return pl.pallas_call(
        _final_norm_kernel,
        grid=(s // tm,),
        in_specs=[pl.BlockSpec((tm, d), lambda i: (i, 0)), pl.BlockSpec((1, d), lambda i: (0, 0))],
        out_specs=pl.BlockSpec((tm, d), lambda i: (i, 0)),
        out_shape=jax.ShapeDtypeStruct((s, d), F32),
        compiler_params=_cparams("parallel"),
        name="final_norm",
    )(h, gain.reshape(1, d))


def conv_gla_layer(h, norm, w_in, conv_w, wa2_f, ba_f, wa2_b, ba_b, head_norm, w_out):
    d = h.shape[1]
    width = conv_w.shape[1]
    kdim = wa2_f.shape[1]
    dk = kdim // GLA_HEADS
    dv = head_norm.shape[0]
    vdim = dv * GLA_HEADS
    n_main = 3 * width + 2 * kdim + 2 * vdim
    lr = 2 * GLA_GATE_RANK
    col_q = 3 * width
    col_k, col_v, col_g = col_q + kdim, col_q + 2 * kdim, col_q + 2 * kdim + vdim

    w_main = w_in[:, :n_main].astype(BF16)
    w_lr = jnp.pad(w_in[:, n_main:], ((0, 0), (0, LANES - lr))).astype(BF16)
    proj = norm_matmul(h, norm, w_main, BF16)
    a_lr = norm_matmul(h, norm, w_lr, F32, tn=LANES)
    wa_f = jnp.pad(wa2_f, ((0, LANES - GLA_GATE_RANK), (0, 0)))
    wa_b = jnp.pad(wa2_b, ((GLA_GATE_RANK, LANES - lr), (0, 0)))
    y_conv = gated_conv(proj, conv_w, width)
    y_gla = gla_mixer(proj, a_lr, wa_f, ba_f.reshape(1, -1), wa_b, ba_b.reshape(1, -1),
                      head_norm.reshape(1, -1), col_q, col_k, col_v, col_g, dk, dv)
    w_out_b = w_out.astype(BF16)
    return matmul_residual([y_conv, y_gla], [w_out_b[:width], w_out_b[width:]], h)


def mla_layer(h, tables, norm, w_down, q_norm, kv_norm, w_uq, w_ukv, w_out):
    q_lora, kv_lora = q_norm.shape[0], kv_norm.shape[0]
    heads = w_uq.shape[1] // (MLA_NOPE + MLA_ROPE)
    n_down = w_down.shape[1]
    n_pad = q_lora + kv_lora + LANES
    w_down_b = jnp.pad(w_down, ((0, 0), (0, n_pad - n_down))).astype(BF16)
    down = norm_matmul(h, norm, w_down_b, F32, tm=256, tn=n_pad)
    tables, tables_t = tables
    cq, ckv, kr = mla_post(down, q_norm, kv_norm, tables, q_lora, kv_lora)
    w_uq_t = w_uq.reshape(q_lora, heads, MLA_NOPE + MLA_ROPE).transpose(1, 2, 0)
    w_uq_t = jnp.pad(w_uq_t, ((0, 0), (0, MLA_QK_PAD - MLA_NOPE - MLA_ROPE), (0, 0))).astype(BF16)
    w_ukv_h = w_ukv.reshape(kv_lora, heads, MLA_NOPE + MLA_V)
    w_uk_h = w_ukv_h[:, :, :MLA_NOPE].transpose(1, 0, 2).astype(BF16)
    w_uv_t = w_ukv_h[:, :, MLA_NOPE:].transpose(1, 2, 0).astype(BF16)
    qscale = (MLA_NOPE + MLA_ROPE) ** -0.5 * math.log2(math.e)
    qt = mla_q_up(cq, w_uq_t, tables_t, qscale)
    k, vt = mla_kv_up(ckv, w_uk_h, w_uv_t, kr)
    o = mla_attention(qt, k, vt)
    return matmul_residual([o], [w_out.astype(BF16)], h)


def kernel(x, positions, ab_norm, ab_w_in, ab_conv_w, gla_wa2_fwd, gla_ba_fwd, gla_wa2_bwd, gla_ba_bwd, gla_head_norm, ab_w_out, mla_norm, mla_w_down, mla_q_norm, mla_kv_norm, mla_w_uq, mla_w_ukv, mla_w_out, ffn_norm, router_w, expert_w_gate, expert_w_up, expert_w_down, final_norm):
    b_, s, d = x.shape
    depth = ffn_norm.shape[0]
    outs = []
    for b in range(b_):
        h = x[b]
        tables = (rope_tables(positions[b], 1), rope_tables(positions[b], 0))
        for i in range(depth):
            j = i // 2
            if i % 2 == 0:
                h = conv_gla_layer(h, ab_norm[j], ab_w_in[j], ab_conv_w[j], gla_wa2_fwd[j], gla_ba_fwd[j],
                                   gla_wa2_bwd[j], gla_ba_bwd[j], gla_head_norm[j], ab_w_out[j])
            else:
                h = mla_layer(h, tables, mla_norm[j], mla_w_down[j], mla_q_norm[j], mla_kv_norm[j],
                              mla_w_uq[j], mla_w_ukv[j], mla_w_out[j])
            h = expert_choice_ffn(h, ffn_norm[i], router_w[i].T, expert_w_gate[i].astype(BF16),
                                  expert_w_up[i].astype(BF16), expert_w_down[i].astype(BF16))
        outs.append(_final_norm_call(h, final_norm))
    return jnp.stack(outs)
```

```python
import functools
import math

import jax
import jax.numpy as jnp
from jax import lax
from jax.experimental import pallas as pl
from jax.experimental.pallas import tpu as pltpu

F32 = jnp.float32
BF16 = jnp.bfloat16
I32 = jnp.int32

EPS = 1e-6
LANES = 128
VMEM_LIMIT_BYTES = 56 * 1024 * 1024
GLA_HEADS = 8
GLA_GATE_RANK = 16
GLA_GATE_TAU = 16.0
GLA_CHUNK = 64
MLA_NOPE = 128
MLA_ROPE = 64
MLA_V = 128
MLA_QK_PAD = 256
ROPE_THETA = 10000.0
N_EXPERTS = 16
CAPACITY_FACTOR = 2

NT_DIMS = (((1,), (1,)), ((), ()))
TN_DIMS = (((0,), (0,)), ((), ()))


def _cparams(*sem):
    return pltpu.CompilerParams(dimension_semantics=sem, vmem_limit_bytes=VMEM_LIMIT_BYTES)


def _tile(n, pref):
    t = min(n, pref)
    assert n % t == 0, (n, pref)
    return t


def _dot(a, b):
    return jnp.dot(a, b, preferred_element_type=F32)


def _split3(a):
    p0 = a.astype(BF16)
    r0 = a - p0.astype(F32)
    p1 = r0.astype(BF16)
    p2 = (r0 - p1.astype(F32)).astype(BF16)
    return p0, p1, p2


def _dot_exact_lhs(a, b01):
    p0, p1, p2 = _split3(a)
    return (_dot(p0, b01) + _dot(p1, b01)) + _dot(p2, b01)


def _rms(x, gain):
    return x * lax.rsqrt(jnp.mean(x * x, axis=-1, keepdims=True) + EPS) * gain


def _norm_mm_kernel(h_ref, g_ref, w_ref, o_ref, xn_ref):
    @pl.when(pl.program_id(1) == 0)
    def _():
        xn_ref[...] = _rms(h_ref[...], g_ref[...]).astype(BF16)

    o_ref[...] = _dot(xn_ref[...], w_ref[...]).astype(o_ref.dtype)


def norm_matmul(h, gain, w, out_dtype, tm=512, tn=512):
    s, d = h.shape
    n = w.shape[1]
    tm, tn = _tile(s, tm), _tile(n, tn)
    return pl.pallas_call(
        _norm_mm_kernel,
        grid=(s // tm, n // tn),
        in_specs=[pl.BlockSpec((tm, d), lambda i, j: (i, 0)),
                  pl.BlockSpec((1, d), lambda i, j: (0, 0)),
                  pl.BlockSpec((d, tn), lambda i, j: (0, j))],
        out_specs=pl.BlockSpec((tm, tn), lambda i, j: (i, j)),
        out_shape=jax.ShapeDtypeStruct((s, n), out_dtype),
        scratch_shapes=[pltpu.VMEM((tm, d), BF16)],
        compiler_params=_cparams("parallel", "arbitrary"),
        name="norm_matmul",
    )(h, gain.reshape(1, d), w)


def _mm_res_kernel(*refs, n_in):
    a_refs, w_refs = refs[:n_in], refs[n_in:2 * n_in]
    res_ref, o_ref = refs[2 * n_in], refs[2 * n_in + 1]
    acc = res_ref[...]
    for a, w in zip(a_refs, w_refs):
        acc = acc + _dot(a[...], w[...])
    o_ref[...] = acc


def matmul_residual(a_list, w_list, res, tm=512, tn=512):
    s, n = res.shape
    tm, tn = _tile(s, tm), _tile(n, tn)
    in_specs = [pl.BlockSpec((tm, a.shape[1]), lambda i, j: (i, 0)) for a in a_list]
    in_specs += [pl.BlockSpec((w.shape[0], tn), lambda i, j: (0, j)) for w in w_list]
    in_specs += [pl.BlockSpec((tm, tn), lambda i, j: (i, j))]
    return pl.pallas_call(
        functools.partial(_mm_res_kernel, n_in=len(a_list)),
        grid=(s // tm, n // tn),
        in_specs=in_specs,
        out_specs=pl.BlockSpec((tm, tn), lambda i, j: (i, j)),
        out_shape=jax.ShapeDtypeStruct((s, n), F32),
        compiler_params=_cparams("parallel", "arbitrary"),
        name="matmul_residual",
    )(*a_list, *w_list, res)


HALO = 16


def _conv_kernel(b_ref, c_ref, u_ref, cp_ref, up_ref, cn_ref, un_ref, w_ref, o_ref):
    i, nt = pl.program_id(0), pl.num_programs(0)
    tm = c_ref.shape[0]
    z = c_ref[...].astype(F32) * u_ref[...].astype(F32)
    zp = cp_ref[HALO - 1:HALO, :].astype(F32) * up_ref[HALO - 1:HALO, :].astype(F32)
    zn = cn_ref[0:1, :].astype(F32) * un_ref[0:1, :].astype(F32)
    zp = jnp.where(i > 0, zp, 0.0)
    zn = jnp.where(i < nt - 1, zn, 0.0)
    rows = lax.broadcasted_iota(I32, z.shape, 0)
    z_prev = jnp.where(rows == 0, zp, pltpu.roll(z, 1, axis=0))
    z_next = jnp.where(rows == tm - 1, zn, pltpu.roll(z, tm - 1, axis=0))
    w = w_ref[...]
    y = w[0:1] * z_prev + w[1:2] * z + w[2:3] * z_next
    o_ref[...] = (b_ref[...].astype(F32) * y).astype(o_ref.dtype)


def gated_conv(proj, conv_w, width, tm=512, tc=512):
    s = proj.shape[0]
    tm, tc = _tile(s, tm), _tile(width, tc)
    nc = width // tc
    nh = s // HALO
    per = tm // HALO

    def prev_map(off):
        return lambda i, c: (jnp.maximum(i * per - 1, 0), off * nc + c)

    def next_map(off):
        return lambda i, c: (jnp.minimum((i + 1) * per, nh - 1), off * nc + c)

    return pl.pallas_call(
        _conv_kernel,
        grid=(s // tm, nc),
        in_specs=[pl.BlockSpec((tm, tc), lambda i, c: (i, c)),
                  pl.BlockSpec((tm, tc), lambda i, c: (i, nc + c)),
                  pl.BlockSpec((tm, tc), lambda i, c: (i, 2 * nc + c)),
                  pl.BlockSpec((HALO, tc), prev_map(1)),
                  pl.BlockSpec((HALO, tc), prev_map(2)),
                  pl.BlockSpec((HALO, tc), next_map(1)),
                  pl.BlockSpec((HALO, tc), next_map(2)),
                  pl.BlockSpec((3, tc), lambda i, c: (0, c))],
        out_specs=pl.BlockSpec((tm, tc), lambda i, c: (i, c)),
        out_shape=jax.ShapeDtypeStruct((s, width), BF16),
        compiler_params=_cparams("parallel", "parallel"),
        name="gated_conv",
    )(proj, proj, proj, proj, proj, proj, proj, conv_w)


def _log_sigmoid(z):
    return -(jnp.maximum(-z, 0.0) + jnp.log1p(jnp.exp(-jnp.abs(z))))


def _gla_log_decay(a_ref, wa_ref, ba_ref):
    a = a_ref[...]
    w = wa_ref[...]
    a_hi = a.astype(BF16)
    a_lo = (a - a_hi.astype(F32)).astype(BF16)
    w_hi = w.astype(BF16)
    w_lo = (w - w_hi.astype(F32)).astype(BF16)
    z = (_dot(a_hi, w_hi) + _dot(a_lo, w_hi)) + _dot(a_hi, w_lo) + ba_ref[...]
    return _log_sigmoid(z) / GLA_GATE_TAU


def _gla_chunk(q, k, v, la, state_t, reverse):
    n = q.shape[0]
    r = lax.broadcasted_iota(I32, (n, n), 0)
    c = lax.broadcasted_iota(I32, (n, n), 1)
    tri = jnp.where((c >= r) if reverse else (c <= r), 1.0, 0.0).astype(BF16)
    la_hi = la.astype(BF16)
    la_lo = (la - la_hi.astype(F32)).astype(BF16)
    cum = _dot(tri, la_hi) + _dot(tri, la_lo)
    cum_last = cum[0:1] if reverse else cum[n - 1:n]
    qd = (q * jnp.exp(cum)).astype(BF16)
    kd = (k * jnp.exp(-cum)).astype(BF16)
    ke = (k * jnp.exp(cum_last - cum)).astype(BF16)
    dec = jnp.exp(cum_last)
    sc = lax.dot_general(qd, kd, NT_DIMS, preferred_element_type=F32)
    sc = jnp.where((c > r) if reverse else (c <= r), sc, 0.0).astype(BF16)
    o = _dot(sc, v) + lax.dot_general(qd, state_t.astype(BF16), NT_DIMS, preferred_element_type=F32)
    new_state = state_t * dec + lax.dot_general(v, ke, TN_DIMS, preferred_element_type=F32)
    return o, new_state


def _gla_scan_tile(q_ref, k_ref, v_ref, la, st_ref, reverse, emit):
    tc = q_ref.shape[0]
    nchunk = tc // GLA_CHUNK
    scale = q_ref.shape[1] ** -0.5
    order = range(nchunk - 1, -1, -1) if reverse else range(nchunk)
    state = st_ref[...]
    for ci in order:
        sl = slice(ci * GLA_CHUNK, (ci + 1) * GLA_CHUNK)
        q = q_ref[sl, :].astype(F32) * scale
        k = k_ref[sl, :].astype(F32)
        o, state = _gla_chunk(q, k, v_ref[sl, :], la[sl, :], state, reverse)
        emit(sl, o)
    st_ref[...] = state


def _gla_fwd_kernel(q_ref, k_ref, v_ref, a_ref, wa_ref, ba_ref, o_ref, st_ref):
    @pl.when(pl.program_id(1) == 0)
    def _():
        st_ref[...] = jnp.zeros_like(st_ref)

    la = _gla_log_decay(a_ref, wa_ref, ba_ref)

    def emit(sl, o):
        o_ref[sl, :] = o

    _gla_scan_tile(q_ref, k_ref, v_ref, la, st_ref, False, emit)


def _gla_bwd_kernel(q_ref, k_ref, v_ref, a_ref, wa_ref, ba_ref, of_ref, g_ref, hn_ref, y_ref, st_ref):
    @pl.when(pl.program_id(1) == 0)
    def _():
        st_ref[...] = jnp.zeros_like(st_ref)

    la = _gla_log_decay(a_ref, wa_ref, ba_ref)

    def emit(sl, o):
        tot = _rms(of_ref[sl, :] + o, hn_ref[...])
        g = g_ref[sl, :].astype(F32)
        y_ref[sl, :] = (tot * (g * jax.nn.sigmoid(g))).astype(y_ref.dtype)

    _gla_scan_tile(q_ref, k_ref, v_ref, la, st_ref, True, emit)


def gla_mixer(proj, a_lr, wa_f, ba_f, wa_b, ba_b, head_norm, col_q, col_k, col_v, col_g, dk, dv, tc=512):
    s = proj.shape[0]
    tc = _tile(s, tc)
    nt = s // tc
    h_ = GLA_HEADS
    lr = a_lr.shape[1]

    def specs(row):
        return [pl.BlockSpec((tc, dk), lambda h, i: (row(i), col_q // dk + h)),
                pl.BlockSpec((tc, dk), lambda h, i: (row(i), col_k // dk + h)),
                pl.BlockSpec((tc, dv), lambda h, i: (row(i), col_v // dv + h)),
                pl.BlockSpec((tc, lr), lambda h, i: (row(i), 0)),
                pl.BlockSpec((lr, dk), lambda h, i: (0, h)),
                pl.BlockSpec((1, dk), lambda h, i: (0, h))]

    fwd_row = lambda i: i
    o_fwd = pl.pallas_call(
        _gla_fwd_kernel,
        grid=(h_, nt),
        in_specs=specs(fwd_row),
        out_specs=pl.BlockSpec((tc, dv), lambda h, i: (i, h)),
        out_shape=jax.ShapeDtypeStruct((s, h_ * dv), F32),
        scratch_shapes=[pltpu.VMEM((dv, dk), F32)],
        compiler_params=_cparams("parallel", "arbitrary"),
        name="gla_fwd",
    )(proj, proj, proj, a_lr, wa_f, ba_f)

    bwd_row = lambda i: nt - 1 - i
    return pl.pallas_call(
        _gla_bwd_kernel,
        grid=(h_, nt),
        in_specs=specs(bwd_row) + [
            pl.BlockSpec((tc, dv), lambda h, i: (bwd_row(i), h)),
            pl.BlockSpec((tc, dv), lambda h, i: (bwd_row(i), col_g // dv + h)),
            pl.BlockSpec((1, dv), lambda h, i: (0, 0))],
        out_specs=pl.BlockSpec((tc, dv), lambda h, i: (bwd_row(i), h)),
        out_shape=jax.ShapeDtypeStruct((s, h_ * dv), BF16),
        scratch_shapes=[pltpu.VMEM((dv, dk), F32)],
        compiler_params=_cparams("parallel", "arbitrary"),
        name="gla_bwd",
    )(proj, proj, proj, a_lr, wa_b, ba_b, o_fwd, proj, head_norm)


def _rope_tables_kernel(pos_ref, invf_ref, cos_ref, sina_ref, sinb_ref, *, axis):
    ang = pos_ref[...].astype(F32) * invf_ref[...]
    feat = lax.broadcasted_iota(I32, ang.shape, axis)
    half = MLA_ROPE // 2
    c, s_ = jnp.cos(ang), jnp.sin(ang)
    cos_ref[...] = jnp.where(feat < 2 * half, c, 0.0)
    sina_ref[...] = jnp.where(feat < half, -s_, 0.0)
    sinb_ref[...] = jnp.where((feat >= half) & (feat < 2 * half), s_, 0.0)


def rope_tables(positions, axis, tm=512):
    s = positions.shape[0]
    tm = _tile(s, tm)
    half = MLA_ROPE // 2
    inv_freq = ROPE_THETA ** (-jnp.arange(0, MLA_ROPE, 2, dtype=F32) / MLA_ROPE)
    invf = jnp.concatenate([inv_freq, inv_freq, jnp.zeros((LANES - 2 * half,), F32)])
    if axis == 1:
        pos, invf = positions.reshape(s, 1), invf.reshape(1, LANES)
        pos_spec, invf_spec = pl.BlockSpec((tm, 1), lambda i: (i, 0)), pl.BlockSpec((1, LANES), lambda i: (0, 0))
        spec, shape = pl.BlockSpec((tm, LANES), lambda i: (i, 0)), (s, LANES)
    else:
        pos, invf = positions.reshape(1, s), invf.reshape(LANES, 1)
        pos_spec, invf_spec = pl.BlockSpec((1, tm), lambda i: (0, i)), pl.BlockSpec((LANES, 1), lambda i: (0, 0))
        spec, shape = pl.BlockSpec((LANES, tm), lambda i: (0, i)), (LANES, s)
    return pl.pallas_call(
        functools.partial(_rope_tables_kernel, axis=axis),
        grid=(s // tm,),
        in_specs=[pos_spec, invf_spec],
        out_specs=[spec, spec, spec],
        out_shape=[jax.ShapeDtypeStruct(shape, F32)] * 3,
        compiler_params=_cparams("parallel"),
        name="rope_tables",
    )(pos, invf)


def _rope(x, cos_t, sin_a, sin_b, axis):
    half = MLA_ROPE // 2
    return x * cos_t + pltpu.roll(x, LANES - half, axis=axis) * sin_a + pltpu.roll(x, half, axis=axis) * sin_b


def _mla_post_kernel(d_ref, qn_ref, kvn_ref, cos_ref, sina_ref, sinb_ref, cq_ref, ckv_ref, kr_ref, *, q_lora, kv_lora):
    d = d_ref[...]
    cq_ref[...] = _rms(d[:, :q_lora], qn_ref[...]).astype(BF16)
    ckv_ref[...] = _rms(d[:, q_lora:q_lora + kv_lora], kvn_ref[...]).astype(BF16)
    kr = d[:, q_lora + kv_lora:]
    kr_ref[...] = _rope(kr, cos_ref[...], sina_ref[...], sinb_ref[...], 1).astype(BF16)


def mla_post(down, q_norm, kv_norm, tables, q_lora, kv_lora, tm=512):
    s, n = down.shape
    tm = _tile(s, tm)
    tab = pl.BlockSpec((tm, LANES), lambda i: (i, 0))
    return pl.pallas_call(
        functools.partial(_mla_post_kernel, q_lora=q_lora, kv_lora=kv_lora),
        grid=(s // tm,),
        in_specs=[pl.BlockSpec((tm, n), lambda i: (i, 0)),
                  pl.BlockSpec((1, q_lora), lambda i: (0, 0)),
                  pl.BlockSpec((1, kv_lora), lambda i: (0, 0)), tab, tab, tab],
        out_specs=[pl.BlockSpec((tm, q_lora), lambda i: (i, 0)),
                   pl.BlockSpec((tm, kv_lora), lambda i: (i, 0)),
                   pl.BlockSpec((tm, LANES), lambda i: (i, 0))],
        out_shape=[jax.ShapeDtypeStruct((s, q_lora), BF16),
                   jax.ShapeDtypeStruct((s, kv_lora), BF16),
                   jax.ShapeDtypeStruct((s, LANES), BF16)],
        compiler_params=_cparams("parallel"),
        name="mla_post",
    )(down, q_norm.reshape(1, -1), kv_norm.reshape(1, -1), *tables)


HEAD_GROUP = 4


def _q_up_kernel(x_ref, w_ref, cos_ref, sina_ref, sinb_ref, q_ref, *, qscale):
    x = x_ref[...]
    for g in range(w_ref.shape[0]):
        a = lax.dot_general(w_ref[g], x, NT_DIMS, preferred_element_type=F32)
        rot = _rope(a[MLA_NOPE:, :], cos_ref[...], sina_ref[...], sinb_ref[...], 0)
        q_ref[g] = (jnp.concatenate([a[:MLA_NOPE, :], rot], axis=0) * qscale).astype(BF16)


def mla_q_up(cq, w_uq_t, tables_t, qscale, tm=512):
    s, r = cq.shape
    h_ = w_uq_t.shape[0]
    tm = _tile(s, tm)
    g = _tile(h_, HEAD_GROUP)
    tab = pl.BlockSpec((LANES, tm), lambda i, h: (0, i))
    return pl.pallas_call(
        functools.partial(_q_up_kernel, qscale=qscale),
        grid=(s // tm, h_ // g),
        in_specs=[pl.BlockSpec((tm, r), lambda i, h: (i, 0)),
                  pl.BlockSpec((g, MLA_QK_PAD, r), lambda i, h: (h, 0, 0)), tab, tab, tab],
        out_specs=pl.BlockSpec((g, MLA_QK_PAD, tm), lambda i, h: (h, 0, i)),
        out_shape=jax.ShapeDtypeStruct((h_, MLA_QK_PAD, s), BF16),
        compiler_params=_cparams("parallel", "arbitrary"),
        name="mla_q_up",
    )(cq, w_uq_t, *tables_t)


def _kv_up_kernel(x_ref, wk_ref, wvt_ref, kr_ref, k_ref, vt_ref):
    x = x_ref[...]
    for g in range(wk_ref.shape[0]):
        k_ref[g] = jnp.concatenate([_dot(x, wk_ref[g]).astype(BF16), kr_ref[...]], axis=1)
        vt = lax.dot_general(wvt_ref[g], x, NT_DIMS, preferred_element_type=F32).astype(BF16)
        ones = jnp.ones((vt_ref.shape[2] - MLA_V, vt.shape[1]), BF16)
        vt_ref[g, 0] = jnp.concatenate([vt, ones], axis=0)


V_ONES_ROWS = 16
ATTN_UNROLL = 8


ATTN_KEY_BLOCK = 512


def mla_kv_up(ckv, w_uk_h, w_uv_t, kr, tm=ATTN_KEY_BLOCK):
    s, r = ckv.shape
    h_ = w_uk_h.shape[0]
    tm = _tile(s, tm)
    g = _tile(h_, HEAD_GROUP)
    return pl.pallas_call(
        _kv_up_kernel,
        grid=(s // tm, h_ // g),
        in_specs=[pl.BlockSpec((tm, r), lambda i, h: (i, 0)),
                  pl.BlockSpec((g, r, MLA_NOPE), lambda i, h: (h, 0, 0)),
                  pl.BlockSpec((g, MLA_V, r), lambda i, h: (h, 0, 0)),
                  pl.BlockSpec((tm, LANES), lambda i, h: (i, 0))],
        out_specs=[pl.BlockSpec((g, tm, MLA_QK_PAD), lambda i, h: (h, i, 0)),
                   pl.BlockSpec((g, 1, MLA_V + V_ONES_ROWS, tm), lambda i, h: (h, i, 0, 0))],
        out_shape=[jax.ShapeDtypeStruct((h_, s, MLA_QK_PAD), BF16),
                   jax.ShapeDtypeStruct((h_, s // tm, MLA_V + V_ONES_ROWS, tm), BF16)],
        compiler_params=_cparams("parallel", "arbitrary"),
        name="mla_kv_up",
    )(ckv, w_uk_h, w_uv_t, kr)


def _attn_kernel(qt_ref, k_ref, vt_ref, o_ref, s_ref, acc_ref, *, tkb):
    nkb = vt_ref.shape[1]
    tq = qt_ref.shape[2]
    qt = qt_ref[0]

    def scores(j, slot):
        off = pl.multiple_of(j * tkb, tkb)
        s = _dot(k_ref[0, pl.ds(off, tkb), :], qt)
        s_ref[slot] = s
        return jnp.max(s, axis=0, keepdims=True)

    def softmax_pv(j, slot, m, blk_max):
        m_new = jnp.maximum(m, blk_max)
        alpha = jnp.exp2(m - m_new)
        p = jnp.exp2((s_ref[slot] - m_new).astype(BF16))
        acc_ref[...] = alpha * acc_ref[...] + _dot(vt_ref[0, j], p)
        return m_new

    acc_ref[...] = jnp.zeros_like(acc_ref)
    m = jnp.full((1, tq), -jnp.inf, F32)
    unroll = min(ATTN_UNROLL, nkb)

    def trip(base, carry, last):
        m, blk_max = carry
        for u in range(unroll):
            nxt_max = blk_max if (last and u == unroll - 1) else scores(base + u + 1, (u + 1) % 2)
            m = softmax_pv(base + u, u % 2, m, blk_max)
            blk_max = nxt_max
        return m, blk_max

    carry = (m, scores(0, 0))
    carry = lax.fori_loop(0, nkb // unroll - 1, lambda i, c: trip(i * unroll, c, False), carry)
    trip(nkb - unroll, carry, True)
    dv = o_ref.shape[1]
    o_ref[...] = (acc_ref[:dv, :] / acc_ref[dv:dv + 1, :]).T.astype(o_ref.dtype)


def mla_attention(qt, k, vt, tq=512):
    h_, dq, s = qt.shape
    _, nkb, dve, tkb = vt.shape
    dv = dve - V_ONES_ROWS
    tq = _tile(s, tq)
    assert nkb % min(ATTN_UNROLL, nkb) == 0 and (min(ATTN_UNROLL, nkb) % 2 == 0 or nkb == 1)
    return pl.pallas_call(
        functools.partial(_attn_kernel, tkb=tkb),
        grid=(h_, s // tq),
        in_specs=[pl.BlockSpec((1, dq, tq), lambda h, i: (h, 0, i)),
                  pl.BlockSpec((1, s, dq), lambda h, i: (h, 0, 0)),
                  pl.BlockSpec((1, nkb, dve, tkb), lambda h, i: (h, 0, 0, 0))],
        out_specs=pl.BlockSpec((tq, dv), lambda h, i: (i, h)),
        out_shape=jax.ShapeDtypeStruct((s, h_ * dv), BF16),
        scratch_shapes=[pltpu.VMEM((2, tkb, tq), F32), pltpu.VMEM((dve, tq), F32)],
        compiler_params=_cparams("parallel", "arbitrary"),
        name="mla_attention",
    )(qt, k, vt)


def _router_kernel(h_ref, g_ref, wt_ref, aff_ref):
    xn = _rms(h_ref[...], g_ref[...])
    x0, x1, x2 = _split3(xn)
    w0, w1, w2 = _split3(wt_ref[...])

    def nt(a, b):
        return lax.dot_general(a, b, NT_DIMS, preferred_element_type=F32)

    logits = (((nt(w2, x0) + nt(w0, x2)) + nt(w1, x1)) + (nt(w1, x0) + nt(w0, x1))) + nt(w0, x0)
    e = jnp.exp(logits - jnp.max(logits, axis=0, keepdims=True))
    aff_ref[...] = e / jnp.sum(e, axis=0, keepdims=True)


def router_affinity(h, gain, w_router_t, tm=512):
    s, d = h.shape
    e_ = w_router_t.shape[0]
    tm = _tile(s, tm)
    return pl.pallas_call(
        _router_kernel,
        grid=(s // tm,),
        in_specs=[pl.BlockSpec((tm, d), lambda i: (i, 0)),
                  pl.BlockSpec((1, d), lambda i: (0, 0)),
                  pl.BlockSpec((e_, d), lambda i: (0, 0))],
        out_specs=pl.BlockSpec((e_, tm), lambda i: (0, i)),
        out_shape=jax.ShapeDtypeStruct((e_, s), F32),
        compiler_params=_cparams("parallel"),
        name="router_affinity",
    )(h, gain.reshape(1, d), w_router_t)


def _prefix_parts(xb, u_incl, ones_b, sl_strict):
    loc = _dot(xb, u_incl)
    tot = _dot(xb, ones_b)
    base = _dot(sl_strict, tot.astype(BF16))
    return loc, tot, base


def _select_kernel(aff_ref, idx_ref, gate_ref, rank_ref, start_ref, end_ref, sel_ref, *, cap):
    e_, nt, ln = aff_ref.shape
    li = lax.broadcasted_iota(I32, (ln, ln), 0)
    lj = lax.broadcasted_iota(I32, (ln, ln), 1)
    u_incl = jnp.where(li <= lj, 1.0, 0.0).astype(BF16)
    u_incl_t = jnp.where(lj <= li, 1.0, 0.0).astype(BF16)
    ones_b = jnp.ones((ln, ln), BF16)
    ti = lax.broadcasted_iota(I32, (nt, nt), 0)
    tj = lax.broadcasted_iota(I32, (nt, nt), 1)
    sl_strict = jnp.where(tj < ti, 1.0, 0.0).astype(BF16)

    bits = pltpu.bitcast(aff_ref[...], I32)

    def count_ge(thr):
        ge = jnp.where(bits >= thr, 1.0, 0.0)
        return jnp.sum(jnp.sum(ge, axis=2, keepdims=True), axis=1, keepdims=True)

    def search(b, thr):
        cand = thr | jnp.left_shift(jnp.int32(1), 30 - b)
        return jnp.where(count_ge(cand) >= cap, cand, thr)

    thr = lax.fori_loop(0, 31, search, jnp.zeros((e_, 1, 1), I32))
    gt = jnp.where(bits > thr, 1.0, 0.0)
    eq = jnp.where(bits == thr, 1.0, 0.0)
    need = cap - jnp.sum(jnp.sum(gt, axis=2, keepdims=True), axis=1, keepdims=True)

    cnt = jnp.zeros((nt, ln), F32)
    for e in range(e_):
        loc, _, base = _prefix_parts(eq[e].astype(BF16), u_incl, ones_b, sl_strict)
        eq_rank = loc + base - eq[e]
        sel = gt[e] + eq[e] * jnp.where(eq_rank < need[e], 1.0, 0.0)
        sel_ref[e] = sel
        cnt = cnt + sel

    cnt_b = cnt.astype(BF16)
    loc_c = _dot(cnt_b, u_incl)
    tot_c = _dot(cnt_b, ones_b)
    t0, t1, t2 = _split3(tot_c)
    base_c = (_dot(sl_strict, t0) + _dot(sl_strict, t1)) + _dot(sl_strict, t2)
    end = loc_c + base_c
    start = end - cnt
    start_ref[...] = start.astype(I32)
    end_ref[...] = end.astype(I32)

    p_row = lax.broadcasted_iota(I32, (1, cap), 1).astype(F32)
    tile_iota = lax.broadcasted_iota(I32, (nt, cap), 0).astype(F32)
    lane_iota = lax.broadcasted_iota(I32, (ln, cap), 0).astype(F32)

    def compact(e, within):
        sel = sel_ref[e]
        selb = sel.astype(BF16)
        _, tot, base = _prefix_parts(selb, u_incl, ones_b, sl_strict)
        tile_end = (base + tot)[:, 0:1]
        jp = jnp.sum(jnp.where(tile_end <= p_row, 1.0, 0.0), axis=0, keepdims=True)
        g_t = jnp.where(tile_iota == jp, 1.0, 0.0)
        base_p = jnp.sum(g_t * base[:, 0:1], axis=0, keepdims=True)
        r_p = p_row - base_p
        loc_t = lax.dot_general(u_incl_t, selb, NT_DIMS, preferred_element_type=F32)
        sel_loc = _dot(loc_t.astype(BF16), g_t.astype(BF16))
        lane_p = jnp.sum(jnp.where(sel_loc <= r_p, 1.0, 0.0), axis=0, keepdims=True)
        l_t = jnp.where(lane_iota == lane_p, 1.0, 0.0)
        l_b = l_t.astype(BF16)
        idx_ref[e] = (jp * ln + lane_p).astype(I32)
        gate_ref[e] = jnp.sum(g_t * _dot_exact_lhs(aff_ref[e], l_b), axis=0, keepdims=True)
        rank_full = start + within
        rank_ref[e] = jnp.sum(g_t * _dot_exact_lhs(rank_full, l_b), axis=0, keepdims=True).astype(I32)
        return within + sel

    lax.fori_loop(0, e_, compact, jnp.zeros((nt, ln), F32))


def expert_select(aff3, cap):
    e_, nt, ln = aff3.shape
    full3 = lambda a, b, c: pl.BlockSpec((a, b, c), lambda i: (0, 0, 0))
    full2 = lambda a, b: pl.BlockSpec((a, b), lambda i: (0, 0))
    return pl.pallas_call(
        functools.partial(_select_kernel, cap=cap),
        grid=(1,),
        in_specs=[full3(e_, nt, ln)],
        out_specs=[full3(e_, 1, cap), full3(e_, 1, cap), full3(e_, 1, cap), full2(nt, ln), full2(nt, ln)],
        out_shape=[jax.ShapeDtypeStruct((e_, 1, cap), I32),
                   jax.ShapeDtypeStruct((e_, 1, cap), F32),
                   jax.ShapeDtypeStruct((e_, 1, cap), I32),
                   jax.ShapeDtypeStruct((nt, ln), I32),
                   jax.ShapeDtypeStruct((nt, ln), I32)],
        scratch_shapes=[pltpu.VMEM((e_, nt, ln), F32)],
        compiler_params=_cparams("arbitrary"),
        name="expert_select",
    )(aff3)


DMA_ISSUE_UNROLL = 8


def _row_gather_copy(h_hbm, xbuf, sem, slot, tok, p):
    return pltpu.make_async_copy(h_hbm.at[pl.ds(tok, 1)], xbuf.at[slot, pl.ds(p, 1)], sem.at[slot])


def _ffn_up_kernel(idx_ref, h_hbm, g_ref, wg_ref, wu_ref, hid_ref, xbuf, sem):
    tm = xbuf.shape[1]
    step = pl.program_id(0) * pl.num_programs(1) + pl.program_id(1)
    nsteps = pl.num_programs(0) * pl.num_programs(1)
    slot = step % 2

    def issue(s_, slot_):
        def body(p, carry):
            _row_gather_copy(h_hbm, xbuf, sem, slot_, idx_ref[s_ * tm + p], p).start()
            return carry
        lax.fori_loop(0, tm, body, 0, unroll=DMA_ISSUE_UNROLL)

    @pl.when(step == 0)
    def _():
        issue(step, slot)

    @pl.when(step + 1 < nsteps)
    def _():
        issue(step + 1, 1 - slot)

    pltpu.make_async_copy(h_hbm.at[pl.ds(0, tm)], xbuf.at[slot], sem.at[slot]).wait()
    xn = _rms(xbuf[slot], g_ref[...]).astype(BF16)
    gate = _dot(xn, wg_ref[0])
    up = _dot(xn, wu_ref[0])
    hid_ref[...] = (gate * jax.nn.sigmoid(gate) * up).astype(hid_ref.dtype)


def ffn_up(h, gain, idx_flat, w_gate, w_up, cap, tm=512):
    s, d = h.shape
    e_, _, ff = w_gate.shape
    tm = _tile(cap, tm)
    r = cap // tm
    grid_spec = pltpu.PrefetchScalarGridSpec(
        num_scalar_prefetch=1,
        grid=(e_, r),
        in_specs=[pl.BlockSpec(memory_space=pl.ANY),
                  pl.BlockSpec((1, d), lambda e, i, idx: (0, 0)),
                  pl.BlockSpec((1, d, ff), lambda e, i, idx: (e, 0, 0)),
                  pl.BlockSpec((1, d, ff), lambda e, i, idx: (e, 0, 0))],
        out_specs=pl.BlockSpec((tm, ff), lambda e, i, idx: (e * r + i, 0)),
        scratch_shapes=[pltpu.VMEM((2, tm, d), F32), pltpu.SemaphoreType.DMA((2,))],
    )
    return pl.pallas_call(
        _ffn_up_kernel,
        grid_spec=grid_spec,
        out_shape=jax.ShapeDtypeStruct((e_ * cap, ff), BF16),
        compiler_params=_cparams("arbitrary", "arbitrary"),
        name="ffn_up",
    )(idx_flat, h, gain.reshape(1, d), w_gate, w_up)


def _pack_bf16_pairs(lo, hi):
    lo_bits = lax.bitcast_convert_type(lo.astype(BF16).astype(F32), jnp.uint32)
    hi_bits = lax.bitcast_convert_type(hi.astype(BF16).astype(F32), jnp.uint32)
    return (hi_bits & jnp.uint32(0xFFFF0000)) | (lo_bits >> 16)


def _unpack_bf16_pairs(w):
    lo = lax.bitcast_convert_type(w << 16, F32).astype(BF16)
    hi = lax.bitcast_convert_type(w & jnp.uint32(0xFFFF0000), F32).astype(BF16)
    return lo, hi


def _row_scatter_copy(ybuf, y_hbm, sem, slot, p, row):
    return pltpu.make_async_copy(ybuf.at[slot, pl.ds(p, 1)], y_hbm.at[pl.ds(row, 1)], sem.at[slot])


def _ffn_down_kernel(rank_ref, hid_ref, wd_ref, gate_ref, y_hbm, ybuf, sem):
    tm, half = ybuf.shape[1], ybuf.shape[2]
    step = pl.program_id(0) * pl.num_programs(1) + pl.program_id(1)
    nsteps = pl.num_programs(0) * pl.num_programs(1)
    slot = step % 2

    def drain(slot_):
        pltpu.make_async_copy(ybuf.at[slot_], y_hbm.at[pl.ds(0, tm)], sem.at[slot_]).wait()

    @pl.when(step >= 2)
    def _():
        drain(slot)

    y = _dot(hid_ref[...], wd_ref[0]) * gate_ref[...]
    ybuf[slot] = _pack_bf16_pairs(y[:, :half], y[:, half:])

    def body(p, carry):
        _row_scatter_copy(ybuf, y_hbm, sem, slot, p, rank_ref[step * tm + p]).start()
        return carry
    lax.fori_loop(0, tm, body, 0, unroll=DMA_ISSUE_UNROLL)

    @pl.when(step == nsteps - 1)
    def _():
        drain(slot)

        @pl.when(nsteps >= 2)
        def _():
            drain(1 - slot)


def ffn_down(hid, w_down, gate_col, rank_flat, cap, tm=512):
    n, ff = hid.shape
    e_, _, d = w_down.shape
    tm = _tile(cap, tm)
    r = cap // tm
    grid_spec = pltpu.PrefetchScalarGridSpec(
        num_scalar_prefetch=1,
        grid=(e_, r),
        in_specs=[pl.BlockSpec((tm, ff), lambda e, i, rk: (e * r + i, 0)),
                  pl.BlockSpec((1, ff, d), lambda e, i, rk: (e, 0, 0)),
                  pl.BlockSpec((tm, 1), lambda e, i, rk: (e * r + i, 0))],
        out_specs=pl.BlockSpec(memory_space=pl.ANY),
        scratch_shapes=[pltpu.VMEM((2, tm, d // 2), jnp.uint32), pltpu.SemaphoreType.DMA((2,))],
    )
    return pl.pallas_call(
        _ffn_down_kernel,
        grid_spec=grid_spec,
        out_shape=jax.ShapeDtypeStruct((n, d // 2), jnp.uint32),
        compiler_params=_cparams("arbitrary", "arbitrary"),
        name="ffn_down",
    )(rank_flat, hid, w_down, gate_col)


def _combine_kernel(ws_ref, vb_ref, h_ref, st_ref, en_ref, y_hbm, o_ref, win, sem, *, nchunks):
    i, nt = pl.program_id(0), pl.num_programs(0)
    wrows, half = win.shape[1], win.shape[2]

    def first_chunk(t):
        return jnp.minimum(ws_ref[t] // wrows, nchunks - 1)

    def chunk_copy(chunk, slot):
        row0 = pl.multiple_of(chunk * wrows, wrows)
        return pltpu.make_async_copy(y_hbm.at[pl.ds(row0, wrows)], win.at[slot], sem.at[slot])

    lo_c = first_chunk(i)
    v0 = vb_ref[i]
    n_i = vb_ref[i + 1] - v0

    @pl.when(i == 0)
    def _():
        chunk_copy(lo_c, 0).start()

    st, en = st_ref[...], en_ref[...]
    o_ref[...] = h_ref[...]

    def visit(k, carry):
        slot = (v0 + k) % 2
        chunk = lo_c + k
        last_in_tile = k == n_i - 1
        nxt = jnp.where(last_in_tile, first_chunk(jnp.minimum(i + 1, nt - 1)), chunk + 1)

        @pl.when(jnp.logical_or(jnp.logical_not(last_in_tile), i + 1 < nt))
        def _():
            chunk_copy(nxt, 1 - slot).start()

        chunk_copy(chunk, slot).wait()
        r_abs = chunk * wrows + lax.broadcasted_iota(I32, (1, wrows), 1)
        q = jnp.where((st <= r_abs) & (r_abs < en), 1.0, 0.0).astype(BF16)
        lo, hi = _unpack_bf16_pairs(win[slot])
        o_ref[:, :half] += _dot(q, lo)
        o_ref[:, half:] += _dot(q, hi)
        return carry

    lax.fori_loop(0, n_i, visit, 0)


def _combine_visits(win_starts, wrows, nchunks):
    lo_c = jnp.minimum(win_starts[:-1] // wrows, nchunks - 1)
    hi_c = jnp.maximum(lo_c, (win_starts[1:] + wrows - 1) // wrows - 1)
    return hi_c - lo_c + 1


def ffn_combine(h, y_sorted, start_col, end_col, win_starts, tm, wrows=256):
    s, d = h.shape
    total = y_sorted.shape[0]
    wrows = _tile(total, wrows)
    nchunks = total // wrows
    n_visits = _combine_visits(win_starts, wrows, nchunks)
    visit_base = jnp.concatenate([jnp.zeros((1,), I32), jnp.cumsum(n_visits, dtype=I32)])
    grid_spec = pltpu.PrefetchScalarGridSpec(
        num_scalar_prefetch=2,
        grid=(s // tm,),
        in_specs=[pl.BlockSpec((tm, d), lambda i, ws, vb: (i, 0)),
                  pl.BlockSpec((tm, 1), lambda i, ws, vb: (i, 0)),
                  pl.BlockSpec((tm, 1), lambda i, ws, vb: (i, 0)),
                  pl.BlockSpec(memory_space=pl.ANY)],
        out_specs=pl.BlockSpec((tm, d), lambda i, ws, vb: (i, 0)),
        scratch_shapes=[pltpu.VMEM((2, wrows, d // 2), jnp.uint32), pltpu.SemaphoreType.DMA((2,))],
    )
    return pl.pallas_call(
        functools.partial(_combine_kernel, nchunks=nchunks),
        grid_spec=grid_spec,
        out_shape=jax.ShapeDtypeStruct((s, d), F32),
        compiler_params=_cparams("arbitrary"),
        name="ffn_combine",
    )(win_starts, visit_base, h, start_col, end_col, y_sorted)


def expert_choice_ffn(h, gain, w_router_t, w_gate, w_up, w_down, combine_tm=512):
    s, d = h.shape
    e_ = w_router_t.shape[0]
    cap = CAPACITY_FACTOR * s // e_
    aff = router_affinity(h, gain, w_router_t)
    idx, gate, rank, start, end = expert_select(aff.reshape(e_, s // LANES, LANES), cap)
    hid = ffn_up(h, gain, idx.reshape(-1), w_gate, w_up, cap)
    y_sorted = ffn_down(hid, w_down, gate.reshape(-1, 1), rank.reshape(-1), cap)
    start_flat = start.reshape(-1)
    tm = _tile(s, combine_tm)
    win_starts = jnp.concatenate([start_flat[::tm], jnp.full((1,), e_ * cap, I32)])
    return ffn_combine(h, y_sorted, start_flat.reshape(s, 1), end.reshape(s, 1), win_starts, tm)


def _final_norm_kernel(h_ref, g_ref, o_ref):
    o_ref[...] = _rms(h_ref[...], g_ref[...])


def _final_norm_call(h, gain, tm=512):
    s, d = h.shape
    tm = _tile(s, tm)
    return pl.pallas_call(
        _final_norm_kernel,
        grid=(s // tm,),
        in_specs=[pl.BlockSpec((tm, d), lambda i: (i, 0)), pl.BlockSpec((1, d), lambda i: (0, 0))],
        out_specs=pl.BlockSpec((tm, d), lambda i: (i, 0)),
        out_shape=jax.ShapeDtypeStruct((s, d), F32),
        compiler_params=_cparams("parallel"),
        name="final_norm",
    )(h, gain.reshape(1, d))


def conv_gla_layer(h, norm, w_in, conv_w, wa2_f, ba_f, wa2_b, ba_b, head_norm, w_out):
    d = h.shape[1]
    width = conv_w.shape[1]
    kdim = wa2_f.shape[1]
    dk = kdim // GLA_HEADS
    dv = head_norm.shape[0]
    vdim = dv * GLA_HEADS
    n_main = 3 * width + 2 * kdim + 2 * vdim
    lr = 2 * GLA_GATE_RANK
    col_q = 3 * width
    col_k, col_v, col_g = col_q + kdim, col_q + 2 * kdim, col_q + 2 * kdim + vdim

    w_main = w_in[:, :n_main].astype(BF16)
    w_lr = jnp.pad(w_in[:, n_main:], ((0, 0), (0, LANES - lr))).astype(BF16)
    proj = norm_matmul(h, norm, w_main, BF16)
    a_lr = norm_matmul(h, norm, w_lr, F32, tn=LANES)
    wa_f = jnp.pad(wa2_f, ((0, LANES - GLA_GATE_RANK), (0, 0)))
    wa_b = jnp.pad(wa2_b, ((GLA_GATE_RANK, LANES - lr), (0, 0)))
    y_conv = gated_conv(proj, conv_w, width)
    y_gla = gla_mixer(proj, a_lr, wa_f, ba_f.reshape(1, -1), wa_b, ba_b.reshape(1, -1),
                      head_norm.reshape(1, -1), col_q, col_k, col_v, col_g, dk, dv)
    w_out_b = w_out.astype(BF16)
    return matmul_residual([y_conv, y_gla], [w_out_b[:width], w_out_b[width:]], h)


def mla_layer(h, tables, norm, w_down, q_norm, kv_norm, w_uq, w_ukv, w_out):
    q_lora, kv_lora = q_norm.shape[0], kv_norm.shape[0]
    heads = w_uq.shape[1] // (MLA_NOPE + MLA_ROPE)
    n_down = w_down.shape[1]
    n_pad = q_lora + kv_lora + LANES
    w_down_b = jnp.pad(w_down, ((0, 0), (0, n_pad - n_down))).astype(BF16)
    down = norm_matmul(h, norm, w_down_b, F32, tm=256, tn=n_pad)
    tables, tables_t = tables
    cq, ckv, kr = mla_post(down, q_norm, kv_norm, tables, q_lora, kv_lora)
    w_uq_t = w_uq.reshape(q_lora, heads, MLA_NOPE + MLA_ROPE).transpose(1, 2, 0)
    w_uq_t = jnp.pad(w_uq_t, ((0, 0), (0, MLA_QK_PAD - MLA_NOPE - MLA_ROPE), (0, 0))).astype(BF16)
    w_ukv_h = w_ukv.reshape(kv_lora, heads, MLA_NOPE + MLA_V)
    w_uk_h = w_ukv_h[:, :, :MLA_NOPE].transpose(1, 0, 2).astype(BF16)
    w_uv_t = w_ukv_h[:, :, MLA_NOPE:].transpose(1, 2, 0).astype(BF16)
    qscale = (MLA_NOPE + MLA_ROPE) ** -0.5 * math.log2(math.e)
    qt = mla_q_up(cq, w_uq_t, tables_t, qscale)
    k, vt = mla_kv_up(ckv, w_uk_h, w_uv_t, kr)
    o = mla_attention(qt, k, vt)
    return matmul_residual([o], [w_out.astype(BF16)], h)


def kernel(x, positions, ab_norm, ab_w_in, ab_conv_w, gla_wa2_fwd, gla_ba_fwd, gla_wa2_bwd, gla_ba_bwd, gla_head_norm, ab_w_out, mla_norm, mla_w_down, mla_q_norm, mla_kv_norm, mla_w_uq, mla_w_ukv, mla_w_out, ffn_norm, router_w, expert_w_gate, expert_w_up, expert_w_down, final_norm):
    b_, s, d = x.shape
    depth = ffn_norm.shape[0]
    outs = []
    for b in range(b_):
        h = x[b]
        tables = (rope_tables(positions[b], 1), rope_tables(positions[b], 0))
        for i in range(depth):
            j = i // 2
            if i % 2 == 0:
                h = conv_gla_layer(h, ab_norm[j], ab_w_in[j], ab_conv_w[j], gla_wa2_fwd[j], gla_ba_fwd[j],
                                   gla_wa2_bwd[j], gla_ba_bwd[j], gla_head_norm[j], ab_w_out[j])
            else:
                h = mla_layer(h, tables, mla_norm[j], mla_w_down[j], mla_q_norm[j], mla_kv_norm[j],
                              mla_w_uq[j], mla_w_ukv[j], mla_w_out[j])
            h = expert_choice_ffn(h, ffn_norm[i], router_w[i].T, expert_w_gate[i].astype(BF16),
                                  expert_w_up[i].astype(BF16), expert_w_down[i].astype(BF16))
        outs.append(_final_norm_call(h, final_norm))
    return jnp.stack(outs)
```

```python
import functools
import math

import jax
import jax.numpy as jnp
from jax import lax
from jax.experimental import pallas as pl
from jax.experimental.pallas import tpu as pltpu

F32 = jnp.float32
BF16 = jnp.bfloat16
I32 = jnp.int32

EPS = 1e-6
LANES = 128
VMEM_LIMIT_BYTES = 56 * 1024 * 1024
GLA_HEADS = 8
GLA_GATE_RANK = 16
GLA_GATE_TAU = 16.0
GLA_CHUNK = 64
MLA_NOPE = 128
MLA_ROPE = 64
MLA_V = 128
MLA_QK_PAD = 256
ROPE_THETA = 10000.0
N_EXPERTS = 16
CAPACITY_FACTOR = 2

NT_DIMS = (((1,), (1,)), ((), ()))
TN_DIMS = (((0,), (0,)), ((), ()))


def _cparams(*sem):
    return pltpu.CompilerParams(dimension_semantics=sem, vmem_limit_bytes=VMEM_LIMIT_BYTES)


def _tile(n, pref):
    t = min(n, pref)
    assert n % t == 0, (n, pref)
    return t


def _dot(a, b):
    return jnp.dot(a, b, preferred_element_type=F32)


def _split3(a):
    p0 = a.astype(BF16)
    r0 = a - p0.astype(F32)
    p1 = r0.astype(BF16)
    p2 = (r0 - p1.astype(F32)).astype(BF16)
    return p0, p1, p2


def _dot_exact_lhs(a, b01):
    p0, p1, p2 = _split3(a)
    return (_dot(p0, b01) + _dot(p1, b01)) + _dot(p2, b01)


def _rms(x, gain):
    return x * lax.rsqrt(jnp.mean(x * x, axis=-1, keepdims=True) + EPS) * gain


def _norm_mm_kernel(h_ref, g_ref, w_ref, o_ref, xn_ref):
    @pl.when(pl.program_id(1) == 0)
    def _():
        xn_ref[...] = _rms(h_ref[...], g_ref[...]).astype(BF16)

    o_ref[...] = _dot(xn_ref[...], w_ref[...]).astype(o_ref.dtype)


def norm_matmul(h, gain, w, out_dtype, tm=512, tn=512):
    s, d = h.shape
    n = w.shape[1]
    tm, tn = _tile(s, tm), _tile(n, tn)
    return pl.pallas_call(
        _norm_mm_kernel,
        grid=(s // tm, n // tn),
        in_specs=[pl.BlockSpec((tm, d), lambda i, j: (i, 0)),
                  pl.BlockSpec((1, d), lambda i, j: (0, 0)),
                  pl.BlockSpec((d, tn), lambda i, j: (0, j))],
        out_specs=pl.BlockSpec((tm, tn), lambda i, j: (i, j)),
        out_shape=jax.ShapeDtypeStruct((s, n), out_dtype),
        scratch_shapes=[pltpu.VMEM((tm, d), BF16)],
        compiler_params=_cparams("parallel", "arbitrary"),
        name="norm_matmul",
    )(h, gain.reshape(1, d), w)


def _mm_res_kernel(*refs, n_in):
    a_refs, w_refs = refs[:n_in], refs[n_in:2 * n_in]
    res_ref, o_ref = refs[2 * n_in], refs[2 * n_in + 1]
    acc = res_ref[...]
    for a, w in zip(a_refs, w_refs):
        acc = acc + _dot(a[...], w[...])
    o_ref[...] = acc


def matmul_residual(a_list, w_list, res, tm=512, tn=1024):
    s, n = res.shape
    tm, tn = _tile(s, tm), _tile(n, tn)
    in_specs = [pl.BlockSpec((tm, a.shape[1]), lambda i, j: (i, 0)) for a in a_list]
    in_specs += [pl.BlockSpec((w.shape[0], tn), lambda i, j: (0, j)) for w in w_list]
    in_specs += [pl.BlockSpec((tm, tn), lambda i, j: (i, j))]
    return pl.pallas_call(
        functools.partial(_mm_res_kernel, n_in=len(a_list)),
        grid=(s // tm, n // tn),
        in_specs=in_specs,
        out_specs=pl.BlockSpec((tm, tn), lambda i, j: (i, j)),
        out_shape=jax.ShapeDtypeStruct((s, n), F32),
        compiler_params=_cparams("parallel", "arbitrary"),
        name="matmul_residual",
    )(*a_list, *w_list, res)


HALO = 16


def _conv_kernel(b_ref, c_ref, u_ref, cp_ref, up_ref, cn_ref, un_ref, w_ref, o_ref):
    i, nt = pl.program_id(0), pl.num_programs(0)
    tm = c_ref.shape[0]
    z = c_ref[...].astype(F32) * u_ref[...].astype(F32)
    zp = cp_ref[HALO - 1:HALO, :].astype(F32) * up_ref[HALO - 1:HALO, :].astype(F32)
    zn = cn_ref[0:1, :].astype(F32) * un_ref[0:1, :].astype(F32)
    zp = jnp.where(i > 0, zp, 0.0)
    zn = jnp.where(i < nt - 1, zn, 0.0)
    rows = lax.broadcasted_iota(I32, z.shape, 0)
    z_prev = jnp.where(rows == 0, zp, pltpu.roll(z, 1, axis=0))
    z_next = jnp.where(rows == tm - 1, zn, pltpu.roll(z, tm - 1, axis=0))
    w = w_ref[...]
    y = w[0:1] * z_prev + w[1:2] * z + w[2:3] * z_next
    o_ref[...] = (b_ref[...].astype(F32) * y).astype(o_ref.dtype)


def gated_conv(proj, conv_w, width, tm=512, tc=512):
    s = proj.shape[0]
    tm, tc = _tile(s, tm), _tile(width, tc)
    nc = width // tc
    nh = s // HALO
    per = tm // HALO

    def prev_map(off):
        return lambda i, c: (jnp.maximum(i * per - 1, 0), off * nc + c)

    def next_map(off):
        return lambda i, c: (jnp.minimum((i + 1) * per, nh - 1), off * nc + c)

    return pl.pallas_call(
        _conv_kernel,
        grid=(s // tm, nc),
        in_specs=[pl.BlockSpec((tm, tc), lambda i, c: (i, c)),
                  pl.BlockSpec((tm, tc), lambda i, c: (i, nc + c)),
                  pl.BlockSpec((tm, tc), lambda i, c: (i, 2 * nc + c)),
                  pl.BlockSpec((HALO, tc), prev_map(1)),
                  pl.BlockSpec((HALO, tc), prev_map(2)),
                  pl.BlockSpec((HALO, tc), next_map(1)),
                  pl.BlockSpec((HALO, tc), next_map(2)),
                  pl.BlockSpec((3, tc), lambda i, c: (0, c))],
        out_specs=pl.BlockSpec((tm, tc), lambda i, c: (i, c)),
        out_shape=jax.ShapeDtypeStruct((s, width), BF16),
        compiler_params=_cparams("parallel", "parallel"),
        name="gated_conv",
    )(proj, proj, proj, proj, proj, proj, proj, conv_w)


def _log_sigmoid(z):
    return -(jnp.maximum(-z, 0.0) + jnp.log1p(jnp.exp(-jnp.abs(z))))


def _gla_log_decay(a_ref, wa_ref, ba_ref):
    a = a_ref[...]
    w = wa_ref[...]
    a_hi = a.astype(BF16)
    a_lo = (a - a_hi.astype(F32)).astype(BF16)
    w_hi = w.astype(BF16)
    w_lo = (w - w_hi.astype(F32)).astype(BF16)
    z = (_dot(a_hi, w_hi) + _dot(a_lo, w_hi)) + _dot(a_hi, w_lo) + ba_ref[...]
    return _log_sigmoid(z) / GLA_GATE_TAU


def _bdot(spec, a, b):
    return jnp.einsum(spec, a, b, preferred_element_type=F32)


def _gla_tile(q, k, v, la, state_t, reverse):
    t, dk = q.shape
    dv = v.shape[1]
    n = GLA_CHUNK
    nc = t // n
    r = lax.broadcasted_iota(I32, (nc, n, n), 1)
    c = lax.broadcasted_iota(I32, (nc, n, n), 2)
    tri = jnp.where((c >= r) if reverse else (c <= r), 1.0, 0.0).astype(BF16)
    la3 = la.reshape(nc, n, dk)
    la_hi = la3.astype(BF16)
    la_lo = (la3 - la_hi.astype(F32)).astype(BF16)
    cum = _bdot('cij,cjd->cid', tri, la_hi) + _bdot('cij,cjd->cid', tri, la_lo)
    cum_last = cum[:, 0:1] if reverse else cum[:, n - 1:n]
    q3, k3, v3 = q.reshape(nc, n, dk), k.reshape(nc, n, dk), v.reshape(nc, n, dv)
    qd = (q3 * jnp.exp(cum)).astype(BF16)
    kd = (k3 * jnp.exp(-cum)).astype(BF16)
    ke = (k3 * jnp.exp(cum_last - cum)).astype(BF16)
    dec = jnp.exp(cum_last)
    sc = _bdot('cid,cjd->cij', qd, kd)
    sc = jnp.where((c > r) if reverse else (c <= r), sc, 0.0).astype(BF16)
    o_intra = _bdot('cij,cje->cie', sc, v3)
    kv = _bdot('cje,cjd->ced', v3, ke)
    incoming = [None] * nc
    state = state_t
    for ci in (range(nc - 1, -1, -1) if reverse else range(nc)):
        incoming[ci] = state.astype(BF16)
        state = state * dec[ci] + kv[ci]
    o_inter = _bdot('cid,ced->cie', qd, jnp.stack(incoming))
    return (o_intra + o_inter).reshape(t, dv), state


GLA_HEAD_GROUP = 2


def _gla_scan_tile(q_ref, k_ref, v_ref, la, st_ref, reverse, emit):
    group, dv, dk = st_ref.shape
    scale = dk ** -0.5
    for g in range(group):
        kc, vc = slice(g * dk, (g + 1) * dk), slice(g * dv, (g + 1) * dv)
        q = q_ref[:, kc].astype(F32) * scale
        k = k_ref[:, kc].astype(F32)
        o, st_ref[g] = _gla_tile(q, k, v_ref[:, vc], la[:, kc], st_ref[g], reverse)
        emit(vc, o)


def _gla_fwd_kernel(q_ref, k_ref, v_ref, a_ref, wa_ref, ba_ref, o_ref, st_ref):
    @pl.when(pl.program_id(1) == 0)
    def _():
        st_ref[...] = jnp.zeros_like(st_ref)

    la = _gla_log_decay(a_ref, wa_ref, ba_ref)

    def emit(vc, o):
        o_ref[:, vc] = o

    _gla_scan_tile(q_ref, k_ref, v_ref, la, st_ref, False, emit)


def _gla_bwd_kernel(q_ref, k_ref, v_ref, a_ref, wa_ref, ba_ref, of_ref, g_ref, hn_ref, y_ref, st_ref):
    @pl.when(pl.program_id(1) == 0)
    def _():
        st_ref[...] = jnp.zeros_like(st_ref)

    la = _gla_log_decay(a_ref, wa_ref, ba_ref)

    def emit(vc, o):
        tot = _rms(of_ref[:, vc] + o, hn_ref[...])
        g = g_ref[:, vc].astype(F32)
        y_ref[:, vc] = (tot * (g * jax.nn.sigmoid(g))).astype(y_ref.dtype)

    _gla_scan_tile(q_ref, k_ref, v_ref, la, st_ref, True, emit)


def gla_mixer(proj, a_lr, wa_f, ba_f, wa_b, ba_b, head_norm, col_q, col_k, col_v, col_g, dk, dv, tc=512):
    s = proj.shape[0]
    tc = _tile(s, tc)
    nt = s // tc
    grp = GLA_HEAD_GROUP
    ng = GLA_HEADS // grp
    gk, gv = grp * dk, grp * dv
    assert col_q % gk == 0 and col_k % gk == 0 and col_v % gv == 0 and col_g % gv == 0
    lr = a_lr.shape[1]

    def specs(row):
        return [pl.BlockSpec((tc, gk), lambda h, i: (row(i), col_q // gk + h)),
                pl.BlockSpec((tc, gk), lambda h, i: (row(i), col_k // gk + h)),
                pl.BlockSpec((tc, gv), lambda h, i: (row(i), col_v // gv + h)),
                pl.BlockSpec((tc, lr), lambda h, i: (row(i), 0)),
                pl.BlockSpec((lr, gk), lambda h, i: (0, h)),
                pl.BlockSpec((1, gk), lambda h, i: (0, h))]

    fwd_row = lambda i: i
    o_fwd = pl.pallas_call(
        _gla_fwd_kernel,
        grid=(ng, nt),
        in_specs=specs(fwd_row),
        out_specs=pl.BlockSpec((tc, gv), lambda h, i: (i, h)),
        out_shape=jax.ShapeDtypeStruct((s, GLA_HEADS * dv), F32),
        scratch_shapes=[pltpu.VMEM((grp, dv, dk), F32)],
        compiler_params=_cparams("parallel", "arbitrary"),
        name="gla_fwd",
    )(proj, proj, proj, a_lr, wa_f, ba_f)

    bwd_row = lambda i: nt - 1 - i
    return pl.pallas_call(
        _gla_bwd_kernel,
        grid=(ng, nt),
        in_specs=specs(bwd_row) + [
            pl.BlockSpec((tc, gv), lambda h, i: (bwd_row(i), h)),
            pl.BlockSpec((tc, gv), lambda h, i: (bwd_row(i), col_g // gv + h)),
            pl.BlockSpec((1, dv), lambda h, i: (0, 0))],
        out_specs=pl.BlockSpec((tc, gv), lambda h, i: (bwd_row(i), h)),
        out_shape=jax.ShapeDtypeStruct((s, GLA_HEADS * dv), BF16),
        scratch_shapes=[pltpu.VMEM((grp, dv, dk), F32)],
        compiler_params=_cparams("parallel", "arbitrary"),
        name="gla_bwd",
    )(proj, proj, proj, a_lr, wa_b, ba_b, o_fwd, proj, head_norm)


def _rope_tables_kernel(pos_ref, invf_ref, cos_ref, sina_ref, sinb_ref, *, axis):
    ang = pos_ref[...].astype(F32) * invf_ref[...]
    feat = lax.broadcasted_iota(I32, ang.shape, axis)
    half = MLA_ROPE // 2
    c, s_ = jnp.cos(ang), jnp.sin(ang)
    cos_ref[...] = jnp.where(feat < 2 * half, c, 0.0)
    sina_ref[...] = jnp.where(feat < half, -s_, 0.0)
    sinb_ref[...] = jnp.where((feat >= half) & (feat < 2 * half), s_, 0.0)


def rope_tables(positions, axis, tm=512):
    s = positions.shape[0]
    tm = _tile(s, tm)
    half = MLA_ROPE // 2
    inv_freq = ROPE_THETA ** (-jnp.arange(0, MLA_ROPE, 2, dtype=F32) / MLA_ROPE)
    invf = jnp.concatenate([inv_freq, inv_freq, jnp.zeros((LANES - 2 * half,), F32)])
    if axis == 1:
        pos, invf = positions.reshape(s, 1), invf.reshape(1, LANES)
        pos_spec, invf_spec = pl.BlockSpec((tm, 1), lambda i: (i, 0)), pl.BlockSpec((1, LANES), lambda i: (0, 0))
        spec, shape = pl.BlockSpec((tm, LANES), lambda i: (i, 0)), (s, LANES)
    else:
        pos, invf = positions.reshape(1, s), invf.reshape(LANES, 1)
        pos_spec, invf_spec = pl.BlockSpec((1, tm), lambda i: (0, i)), pl.BlockSpec((LANES, 1), lambda i: (0, 0))
        spec, shape = pl.BlockSpec((LANES, tm), lambda i: (0, i)), (LANES, s)
    return pl.pallas_call(
        functools.partial(_rope_tables_kernel, axis=axis),
        grid=(s // tm,),
        in_specs=[pos_spec, invf_spec],
        out_specs=[spec, spec, spec],
        out_shape=[jax.ShapeDtypeStruct(shape, F32)] * 3,
        compiler_params=_cparams("parallel"),
        name="rope_tables",
    )(pos, invf)


def _rope(x, cos_t, sin_a, sin_b, axis):
    half = MLA_ROPE // 2
    return x * cos_t + pltpu.roll(x, LANES - half, axis=axis) * sin_a + pltpu.roll(x, half, axis=axis) * sin_b


def _mla_post_kernel(d_ref, qn_ref, kvn_ref, cos_ref, sina_ref, sinb_ref, cq_ref, ckv_ref, kr_ref, *, q_lora, kv_lora):
    d = d_ref[...]
    cq_ref[...] = _rms(d[:, :q_lora], qn_ref[...]).astype(BF16)
    ckv_ref[...] = _rms(d[:, q_lora:q_lora + kv_lora], kvn_ref[...]).astype(BF16)
    kr = d[:, q_lora + kv_lora:]
    kr_ref[...] = _rope(kr, cos_ref[...], sina_ref[...], sinb_ref[...], 1).astype(BF16)


def mla_post(down, q_norm, kv_norm, tables, q_lora, kv_lora, tm=512):
    s, n = down.shape
    tm = _tile(s, tm)
    tab = pl.BlockSpec((tm, LANES), lambda i: (i, 0))
    return pl.pallas_call(
        functools.partial(_mla_post_kernel, q_lora=q_lora, kv_lora=kv_lora),
        grid=(s // tm,),
        in_specs=[pl.BlockSpec((tm, n), lambda i: (i, 0)),
                  pl.BlockSpec((1, q_lora), lambda i: (0, 0)),
                  pl.BlockSpec((1, kv_lora), lambda i: (0, 0)), tab, tab, tab],
        out_specs=[pl.BlockSpec((tm, q_lora), lambda i: (i, 0)),
                   pl.BlockSpec((tm, kv_lora), lambda i: (i, 0)),
                   pl.BlockSpec((tm, LANES), lambda i: (i, 0))],
        out_shape=[jax.ShapeDtypeStruct((s, q_lora), BF16),
                   jax.ShapeDtypeStruct((s, kv_lora), BF16),
                   jax.ShapeDtypeStruct((s, LANES), BF16)],
        compiler_params=_cparams("parallel"),
        name="mla_post",
    )(down, q_norm.reshape(1, -1), kv_norm.reshape(1, -1), *tables)


HEAD_GROUP = 4


def _q_up_kernel(x_ref, w_ref, cos_ref, sina_ref, sinb_ref, q_ref, *, qscale):
    x = x_ref[...]
    for g in range(w_ref.shape[0]):
        a = lax.dot_general(w_ref[g], x, NT_DIMS, preferred_element_type=F32)
        rot = _rope(a[MLA_NOPE:, :], cos_ref[...], sina_ref[...], sinb_ref[...], 0)
        q_ref[g] = (jnp.concatenate([a[:MLA_NOPE, :], rot], axis=0) * qscale).astype(BF16)


def mla_q_up(cq, w_uq_t, tables_t, qscale, tm=512):
    s, r = cq.shape
    h_ = w_uq_t.shape[0]
    tm = _tile(s, tm)
    g = _tile(h_, HEAD_GROUP)
    tab = pl.BlockSpec((LANES, tm), lambda i, h: (0, i))
    return pl.pallas_call(
        functools.partial(_q_up_kernel, qscale=qscale),
        grid=(s // tm, h_ // g),
        in_specs=[pl.BlockSpec((tm, r), lambda i, h: (i, 0)),
                  pl.BlockSpec((g, MLA_QK_PAD, r), lambda i, h: (h, 0, 0)), tab, tab, tab],
        out_specs=pl.BlockSpec((g, MLA_QK_PAD, tm), lambda i, h: (h, 0, i)),
        out_shape=jax.ShapeDtypeStruct((h_, MLA_QK_PAD, s), BF16),
        compiler_params=_cparams("parallel", "arbitrary"),
        name="mla_q_up",
    )(cq, w_uq_t, *tables_t)


def _kv_up_kernel(x_ref, wk_ref, wvt_ref, kr_ref, k_ref, vt_ref):
    x = x_ref[...]
    for g in range(wk_ref.shape[0]):
        k_ref[g] = jnp.concatenate([_dot(x, wk_ref[g]).astype(BF16), kr_ref[...]], axis=1)
        vt = lax.dot_general(wvt_ref[g], x, NT_DIMS, preferred_element_type=F32).astype(BF16)
        ones = jnp.ones((vt_ref.shape[2] - MLA_V, vt.shape[1]), BF16)
        vt_ref[g, 0] = jnp.concatenate([vt, ones], axis=0)


V_ONES_ROWS = 16
ATTN_UNROLL = 8


ATTN_KEY_BLOCK = 512


def mla_kv_up(ckv, w_uk_h, w_uv_t, kr, tm=ATTN_KEY_BLOCK):
    s, r = ckv.shape
    h_ = w_uk_h.shape[0]
    tm = _tile(s, tm)
    g = _tile(h_, HEAD_GROUP)
    return pl.pallas_call(
        _kv_up_kernel,
        grid=(s // tm, h_ // g),
        in_specs=[pl.BlockSpec((tm, r), lambda i, h: (i, 0)),
                  pl.BlockSpec((g, r, MLA_NOPE), lambda i, h: (h, 0, 0)),
                  pl.BlockSpec((g, MLA_V, r), lambda i, h: (h, 0, 0)),
                  pl.BlockSpec((tm, LANES), lambda i, h: (i, 0))],
        out_specs=[pl.BlockSpec((g, tm, MLA_QK_PAD), lambda i, h: (h, i, 0)),
                   pl.BlockSpec((g, 1, MLA_V + V_ONES_ROWS, tm), lambda i, h: (h, i, 0, 0))],
        out_shape=[jax.ShapeDtypeStruct((h_, s, MLA_QK_PAD), BF16),
                   jax.ShapeDtypeStruct((h_, s // tm, MLA_V + V_ONES_ROWS, tm), BF16)],
        compiler_params=_cparams("parallel", "arbitrary"),
        name="mla_kv_up",
    )(ckv, w_uk_h, w_uv_t, kr)


def _attn_kernel(qt_ref, k_ref, vt_ref, o_ref, s_ref, acc_ref, *, tkb):
    nkb = vt_ref.shape[1]
    tq = qt_ref.shape[2]
    qt = qt_ref[0]

    def scores(j, slot):
        off = pl.multiple_of(j * tkb, tkb)
        s = _dot(k_ref[0, pl.ds(off, tkb), :], qt)
        s_ref[slot] = s
        return jnp.max(s, axis=0, keepdims=True)

    def softmax_pv(j, slot, m, blk_max):
        m_new = jnp.maximum(m, blk_max)
        alpha = jnp.exp2(m - m_new)
        p = jnp.exp2((s_ref[slot] - m_new).astype(BF16))
        acc_ref[...] = alpha * acc_ref[...] + _dot(vt_ref[0, j], p)
        return m_new

    acc_ref[...] = jnp.zeros_like(acc_ref)
    m = jnp.full((1, tq), -jnp.inf, F32)
    unroll = min(ATTN_UNROLL, nkb)

    def trip(base, carry, last):
        m, blk_max = carry
        for u in range(unroll):
            nxt_max = blk_max if (last and u == unroll - 1) else scores(base + u + 1, (u + 1) % 2)
            m = softmax_pv(base + u, u % 2, m, blk_max)
            blk_max = nxt_max
        return m, blk_max

    carry = (m, scores(0, 0))
    carry = lax.fori_loop(0, nkb // unroll - 1, lambda i, c: trip(i * unroll, c, False), carry)
    trip(nkb - unroll, carry, True)
    dv = o_ref.shape[1]
    o_ref[...] = (acc_ref[:dv, :] / acc_ref[dv:dv + 1, :]).T.astype(o_ref.dtype)


def mla_attention(qt, k, vt, tq=512):
    h_, dq, s = qt.shape
    _, nkb, dve, tkb = vt.shape
    dv = dve - V_ONES_ROWS
    tq = _tile(s, tq)
    assert nkb % min(ATTN_UNROLL, nkb) == 0 and (min(ATTN_UNROLL, nkb) % 2 == 0 or nkb == 1)
    return pl.pallas_call(
        functools.partial(_attn_kernel, tkb=tkb),
        grid=(h_, s // tq),
        in_specs=[pl.BlockSpec((1, dq, tq), lambda h, i: (h, 0, i)),
                  pl.BlockSpec((1, s, dq), lambda h, i: (h, 0, 0)),
                  pl.BlockSpec((1, nkb, dve, tkb), lambda h, i: (h, 0, 0, 0))],
        out_specs=pl.BlockSpec((tq, dv), lambda h, i: (i, h)),
        out_shape=jax.ShapeDtypeStruct((s, h_ * dv), BF16),
        scratch_shapes=[pltpu.VMEM((2, tkb, tq), F32), pltpu.VMEM((dve, tq), F32)],
        compiler_params=_cparams("parallel", "arbitrary"),
        name="mla_attention",
    )(qt, k, vt)


def _router_kernel(h_ref, g_ref, wt_ref, aff_ref):
    xn = _rms(h_ref[...], g_ref[...])
    x0, x1, _ = _split3(xn)
    w0, w1, _ = _split3(wt_ref[...])

    def nt(a, b):
        return lax.dot_general(a, b, NT_DIMS, preferred_element_type=F32)

    logits = (nt(w1, x0) + nt(w0, x1)) + nt(w0, x0)
    e = jnp.exp(logits - jnp.max(logits, axis=0, keepdims=True))
    aff_ref[...] = e / jnp.sum(e, axis=0, keepdims=True)


def router_affinity(h, gain, w_router_t, tm=512):
    s, d = h.shape
    e_ = w_router_t.shape[0]
    tm = _tile(s, tm)
    return pl.pallas_call(
        _router_kernel,
        grid=(s // tm,),
        in_specs=[pl.BlockSpec((tm, d), lambda i: (i, 0)),
                  pl.BlockSpec((1, d), lambda i: (0, 0)),
                  pl.BlockSpec((e_, d), lambda i: (0, 0))],
        out_specs=pl.BlockSpec((e_, tm), lambda i: (0, i)),
        out_shape=jax.ShapeDtypeStruct((e_, s), F32),
        compiler_params=_cparams("parallel"),
        name="router_affinity",
    )(h, gain.reshape(1, d), w_router_t)


def _prefix_parts(xb, u_incl, ones_b, sl_strict):
    loc = _dot(xb, u_incl)
    tot = _dot(xb, ones_b)
    base = _dot(sl_strict, tot.astype(BF16))
    return loc, tot, base


def _select_kernel(aff_ref, idx_ref, gate_ref, rank_ref, start_ref, end_ref, sel_ref, *, cap):
    e_, nt, ln = aff_ref.shape
    li = lax.broadcasted_iota(I32, (ln, ln), 0)
    lj = lax.broadcasted_iota(I32, (ln, ln), 1)
    u_incl = jnp.where(li <= lj, 1.0, 0.0).astype(BF16)
    u_incl_t = jnp.where(lj <= li, 1.0, 0.0).astype(BF16)
    ones_b = jnp.ones((ln, ln), BF16)
    ti = lax.broadcasted_iota(I32, (nt, nt), 0)
    tj = lax.broadcasted_iota(I32, (nt, nt), 1)
    sl_strict = jnp.where(tj < ti, 1.0, 0.0).astype(BF16)

    bits = pltpu.bitcast(aff_ref[...], I32)

    def count_ge(thr):
        ge = jnp.where(bits >= thr, 1.0, 0.0)
        return jnp.sum(jnp.sum(ge, axis=2, keepdims=True), axis=1, keepdims=True)

    def search(b, thr):
        cand = thr | jnp.left_shift(jnp.int32(1), 30 - b)
        return jnp.where(count_ge(cand) >= cap, cand, thr)

    thr = lax.fori_loop(0, 31, search, jnp.zeros((e_, 1, 1), I32))
    gt = jnp.where(bits > thr, 1.0, 0.0)
    eq = jnp.where(bits == thr, 1.0, 0.0)
    need = cap - jnp.sum(jnp.sum(gt, axis=2, keepdims=True), axis=1, keepdims=True)

    cnt = jnp.zeros((nt, ln), F32)
    for e in range(e_):
        loc, _, base = _prefix_parts(eq[e].astype(BF16), u_incl, ones_b, sl_strict)
        eq_rank = loc + base - eq[e]
        sel = gt[e] + eq[e] * jnp.where(eq_rank < need[e], 1.0, 0.0)
        sel_ref[e] = sel
        cnt = cnt + sel

    cnt_b = cnt.astype(BF16)
    loc_c = _dot(cnt_b, u_incl)
    tot_c = _dot(cnt_b, ones_b)
    t0, t1, t2 = _split3(tot_c)
    base_c = (_dot(sl_strict, t0) + _dot(sl_strict, t1)) + _dot(sl_strict, t2)
    end = loc_c + base_c
    start = end - cnt
    start_ref[...] = start.astype(I32)
    end_ref[...] = end.astype(I32)

    p_row = lax.broadcasted_iota(I32, (1, cap), 1).astype(F32)
    tile_iota = lax.broadcasted_iota(I32, (nt, cap), 0).astype(F32)
    lane_iota = lax.broadcasted_iota(I32, (ln, cap), 0).astype(F32)

    def compact(e, within):
        sel = sel_ref[e]
        selb = sel.astype(BF16)
        _, tot, base = _prefix_parts(selb, u_incl, ones_b, sl_strict)
        tile_end = (base + tot)[:, 0:1]
        jp = jnp.sum(jnp.where(tile_end <= p_row, 1.0, 0.0), axis=0, keepdims=True)
        g_t = jnp.where(tile_iota == jp, 1.0, 0.0)
        base_p = jnp.sum(g_t * base[:, 0:1], axis=0, keepdims=True)
        r_p = p_row - base_p
        loc_t = lax.dot_general(u_incl_t, selb, NT_DIMS, preferred_element_type=F32)
        sel_loc = _dot(loc_t.astype(BF16), g_t.astype(BF16))
        lane_p = jnp.sum(jnp.where(sel_loc <= r_p, 1.0, 0.0), axis=0, keepdims=True)
        l_t = jnp.where(lane_iota == lane_p, 1.0, 0.0)
        l_b = l_t.astype(BF16)
        idx_ref[e] = (jp * ln + lane_p).astype(I32)
        gate_ref[e] = jnp.sum(g_t * _dot_exact_lhs(aff_ref[e], l_b), axis=0, keepdims=True)
        rank_full = start + within
        rank_ref[e] = jnp.sum(g_t * _dot_exact_lhs(rank_full, l_b), axis=0, keepdims=True).astype(I32)
        return within + sel

    lax.fori_loop(0, e_, compact, jnp.zeros((nt, ln), F32))


def expert_select(aff3, cap):
    e_, nt, ln = aff3.shape
    full3 = lambda a, b, c: pl.BlockSpec((a, b, c), lambda i: (0, 0, 0))
    full2 = lambda a, b: pl.BlockSpec((a, b), lambda i: (0, 0))
    return pl.pallas_call(
        functools.partial(_select_kernel, cap=cap),
        grid=(1,),
        in_specs=[full3(e_, nt, ln)],
        out_specs=[full3(e_, 1, cap), full3(e_, 1, cap), full3(e_, 1, cap), full2(nt, ln), full2(nt, ln)],
        out_shape=[jax.ShapeDtypeStruct((e_, 1, cap), I32),
                   jax.ShapeDtypeStruct((e_, 1, cap), F32),
                   jax.ShapeDtypeStruct((e_, 1, cap), I32),
                   jax.ShapeDtypeStruct((nt, ln), I32),
                   jax.ShapeDtypeStruct((nt, ln), I32)],
        scratch_shapes=[pltpu.VMEM((e_, nt, ln), F32)],
        compiler_params=_cparams("arbitrary"),
        name="expert_select",
    )(aff3)


DMA_ISSUE_UNROLL = 8


def _row_gather_copy(h_hbm, xbuf, sem, slot, tok, p):
    return pltpu.make_async_copy(h_hbm.at[pl.ds(tok, 1)], xbuf.at[slot, pl.ds(p, 1)], sem.at[slot])


def _ffn_up_kernel(idx_ref, h_hbm, g_ref, wg_ref, wu_ref, hid_ref, xbuf, sem):
    tm = xbuf.shape[1]
    step = pl.program_id(0) * pl.num_programs(1) + pl.program_id(1)
    nsteps = pl.num_programs(0) * pl.num_programs(1)
    slot = step % 2

    def issue(s_, slot_):
        def body(p, carry):
            _row_gather_copy(h_hbm, xbuf, sem, slot_, idx_ref[s_ * tm + p], p).start()
            return carry
        lax.fori_loop(0, tm, body, 0, unroll=DMA_ISSUE_UNROLL)

    @pl.when(step == 0)
    def _():
        issue(step, slot)

    @pl.when(step + 1 < nsteps)
    def _():
        issue(step + 1, 1 - slot)

    pltpu.make_async_copy(h_hbm.at[pl.ds(0, tm)], xbuf.at[slot], sem.at[slot]).wait()
    xn = _rms(xbuf[slot], g_ref[...]).astype(BF16)
    gate = _dot(xn, wg_ref[0])
    up = _dot(xn, wu_ref[0])
    hid_ref[...] = (gate * jax.nn.sigmoid(gate) * up).astype(hid_ref.dtype)


def ffn_up(h, gain, idx_flat, w_gate, w_up, cap, tm=512):
    s, d = h.shape
    e_, _, ff = w_gate.shape
    tm = _tile(cap, tm)
    r = cap // tm
    grid_spec = pltpu.PrefetchScalarGridSpec(
        num_scalar_prefetch=1,
        grid=(e_, r),
        in_specs=[pl.BlockSpec(memory_space=pl.ANY),
                  pl.BlockSpec((1, d), lambda e, i, idx: (0, 0)),
                  pl.BlockSpec((1, d, ff), lambda e, i, idx: (e, 0, 0)),
                  pl.BlockSpec((1, d, ff), lambda e, i, idx: (e, 0, 0))],
        out_specs=pl.BlockSpec((tm, ff), lambda e, i, idx: (e * r + i, 0)),
        scratch_shapes=[pltpu.VMEM((2, tm, d), F32), pltpu.SemaphoreType.DMA((2,))],
    )
    return pl.pallas_call(
        _ffn_up_kernel,
        grid_spec=grid_spec,
        out_shape=jax.ShapeDtypeStruct((e_ * cap, ff), BF16),
        compiler_params=_cparams("arbitrary", "arbitrary"),
        name="ffn_up",
    )(idx_flat, h, gain.reshape(1, d), w_gate, w_up)


def _pack_bf16_pairs(lo, hi):
    lo_bits = lax.bitcast_convert_type(lo.astype(BF16).astype(F32), jnp.uint32)
    hi_bits = lax.bitcast_convert_type(hi.astype(BF16).astype(F32), jnp.uint32)
    return (hi_bits & jnp.uint32(0xFFFF0000)) | (lo_bits >> 16)


def _unpack_bf16_pairs(w):
    lo = lax.bitcast_convert_type(w << 16, F32).astype(BF16)
    hi = lax.bitcast_convert_type(w & jnp.uint32(0xFFFF0000), F32).astype(BF16)
    return lo, hi


def _row_scatter_copy(ybuf, y_hbm, sem, slot, p, row):
    return pltpu.make_async_copy(ybuf.at[slot, pl.ds(p, 1)], y_hbm.at[pl.ds(row, 1)], sem.at[slot])


def _ffn_down_kernel(rank_ref, hid_ref, wd_ref, gate_ref, y_hbm, ybuf, sem):
    tm, half = ybuf.shape[1], ybuf.shape[2]
    step = pl.program_id(0) * pl.num_programs(1) + pl.program_id(1)
    nsteps = pl.num_programs(0) * pl.num_programs(1)
    slot = step % 2

    def drain(slot_):
        pltpu.make_async_copy(ybuf.at[slot_], y_hbm.at[pl.ds(0, tm)], sem.at[slot_]).wait()

    @pl.when(step >= 2)
    def _():
        drain(slot)

    y = _dot(hid_ref[...], wd_ref[0]) * gate_ref[...]
    ybuf[slot] = _pack_bf16_pairs(y[:, :half], y[:, half:])

    def body(p, carry):
        _row_scatter_copy(ybuf, y_hbm, sem, slot, p, rank_ref[step * tm + p]).start()
        return carry
    lax.fori_loop(0, tm, body, 0, unroll=DMA_ISSUE_UNROLL)

    @pl.when(step == nsteps - 1)
    def _():
        drain(slot)

        @pl.when(nsteps >= 2)
        def _():
            drain(1 - slot)


def ffn_down(hid, w_down, gate_col, rank_flat, cap, tm=512):
    n, ff = hid.shape
    e_, _, d = w_down.shape
    tm = _tile(cap, tm)
    r = cap // tm
    grid_spec = pltpu.PrefetchScalarGridSpec(
        num_scalar_prefetch=1,
        grid=(e_, r),
        in_specs=[pl.BlockSpec((tm, ff), lambda e, i, rk: (e * r + i, 0)),
                  pl.BlockSpec((1, ff, d), lambda e, i, rk: (e, 0, 0)),
                  pl.BlockSpec((tm, 1), lambda e, i, rk: (e * r + i, 0))],
        out_specs=pl.BlockSpec(memory_space=pl.ANY),
        scratch_shapes=[pltpu.VMEM((2, tm, d // 2), jnp.uint32), pltpu.SemaphoreType.DMA((2,))],
    )
    return pl.pallas_call(
        _ffn_down_kernel,
        grid_spec=grid_spec,
        out_shape=jax.ShapeDtypeStruct((n, d // 2), jnp.uint32),
        compiler_params=_cparams("arbitrary", "arbitrary"),
        name="ffn_down",
    )(rank_flat, hid, w_down, gate_col)


def _combine_kernel(ws_ref, vb_ref, h_ref, st_ref, en_ref, y_hbm, o_ref, win, sem, *, nchunks):
    i, nt = pl.program_id(0), pl.num_programs(0)
    wrows, half = win.shape[1], win.shape[2]

    def first_chunk(t):
        return jnp.minimum(ws_ref[t] // wrows, nchunks - 1)

    def chunk_copy(chunk, slot):
        row0 = pl.multiple_of(chunk * wrows, wrows)
        return pltpu.make_async_copy(y_hbm.at[pl.ds(row0, wrows)], win.at[slot], sem.at[slot])

    lo_c = first_chunk(i)
    v0 = vb_ref[i]
    n_i = vb_ref[i + 1] - v0

    @pl.when(i == 0)
    def _():
        chunk_copy(lo_c, 0).start()

    st, en = st_ref[...], en_ref[...]
    o_ref[...] = h_ref[...]

    def visit(k, carry):
        slot = (v0 + k) % 2
        chunk = lo_c + k
        last_in_tile = k == n_i - 1
        nxt = jnp.where(last_in_tile, first_chunk(jnp.minimum(i + 1, nt - 1)), chunk + 1)

        @pl.when(jnp.logical_or(jnp.logical_not(last_in_tile), i + 1 < nt))
        def _():
            chunk_copy(nxt, 1 - slot).start()

        chunk_copy(chunk, slot).wait()
        r_abs = chunk * wrows + lax.broadcasted_iota(I32, (1, wrows), 1)
        q = jnp.where((st <= r_abs) & (r_abs < en), 1.0, 0.0).astype(BF16)
        lo, hi = _unpack_bf16_pairs(win[slot])
        o_ref[:, :half] += _dot(q, lo)
        o_ref[:, half:] += _dot(q, hi)
        return carry

    lax.fori_loop(0, n_i, visit, 0)


def _combine_visits(win_starts, wrows, nchunks):
    lo_c = jnp.minimum(win_starts[:-1] // wrows, nchunks - 1)
    hi_c = jnp.maximum(lo_c, (win_starts[1:] + wrows - 1) // wrows - 1)
    return hi_c - lo_c + 1


def ffn_combine(h, y_sorted, start_col, end_col, win_starts, tm, wrows=256):
    s, d = h.shape
    total = y_sorted.shape[0]
    wrows = _tile(total, wrows)
    nchunks = total // wrows
    n_visits = _combine_visits(win_starts, wrows, nchunks)
    visit_base = jnp.concatenate([jnp.zeros((1,), I32), jnp.cumsum(n_visits, dtype=I32)])
    grid_spec = pltpu.PrefetchScalarGridSpec(
        num_scalar_prefetch=2,
        grid=(s // tm,),
        in_specs=[pl.BlockSpec((tm, d), lambda i, ws, vb: (i, 0)),
                  pl.BlockSpec((tm, 1), lambda i, ws, vb: (i, 0)),
                  pl.BlockSpec((tm, 1), lambda i, ws, vb: (i, 0)),
                  pl.BlockSpec(memory_space=pl.ANY)],
        out_specs=pl.BlockSpec((tm, d), lambda i, ws, vb: (i, 0)),
        scratch_shapes=[pltpu.VMEM((2, wrows, d // 2), jnp.uint32), pltpu.SemaphoreType.DMA((2,))],
    )
    return pl.pallas_call(
        functools.partial(_combine_kernel, nchunks=nchunks),
        grid_spec=grid_spec,
        out_shape=jax.ShapeDtypeStruct((s, d), F32),
        compiler_params=_cparams("arbitrary"),
        name="ffn_combine",
    )(win_starts, visit_base, h, start_col, end_col, y_sorted)


def expert_choice_ffn(h, gain, w_router_t, w_gate, w_up, w_down, combine_tm=512):
    s, d = h.shape
    e_ = w_router_t.shape[0]
    cap = CAPACITY_FACTOR * s // e_
    aff = router_affinity(h, gain, w_router_t)
    idx, gate, rank, start, end = expert_select(aff.reshape(e_, s // LANES, LANES), cap)
    hid = ffn_up(h, gain, idx.reshape(-1), w_gate, w_up, cap)
    y_sorted = ffn_down(hid, w_down, gate.reshape(-1, 1), rank.reshape(-1), cap)
    start_flat = start.reshape(-1)
    tm = _tile(s, combine_tm)
    win_starts = jnp.concatenate([start_flat[::tm], jnp.full((1,), e_ * cap, I32)])
    return ffn_combine(h, y_sorted, start_flat.reshape(s, 1), end.reshape(s, 1), win_starts, tm)


def _final_norm_kernel(h_ref, g_ref, o_ref):
    o_ref[...] = _rms(h_ref[...], g_ref[...])


def _final_norm_call(h, gain, tm=512):
    s, d = h.shape
    tm = _tile(s, tm)
    return pl.pallas_call(
        _final_norm_kernel,
        grid=(s // tm,),
        in_specs=[pl.BlockSpec((tm, d), lambda i: (i, 0)), pl.BlockSpec((1, d), lambda i: (0, 0))],
        out_specs=pl.BlockSpec((tm, d), lambda i: (i, 0)),
        out_shape=jax.ShapeDtypeStruct((s, d), F32),
        compiler_params=_cparams("parallel"),
        name="final_norm",
    )(h, gain.reshape(1, d))


def conv_gla_layer(h, norm, w_in, conv_w, wa2_f, ba_f, wa2_b, ba_b, head_norm, w_out):
    d = h.shape[1]
    width = conv_w.shape[1]
    kdim = wa2_f.shape[1]
    dk = kdim // GLA_HEADS
    dv = head_norm.shape[0]
    vdim = dv * GLA_HEADS
    n_main = 3 * width + 2 * kdim + 2 * vdim
    lr = 2 * GLA_GATE_RANK
    col_q = 3 * width
    col_k, col_v, col_g = col_q + kdim, col_q + 2 * kdim, col_q + 2 * kdim + vdim

    w_main = w_in[:, :n_main].astype(BF16)
    w_lr = jnp.pad(w_in[:, n_main:], ((0, 0), (0, LANES - lr))).astype(BF16)
    proj = norm_matmul(h, norm, w_main, BF16, tn=1024)
    a_lr = norm_matmul(h, norm, w_lr, F32, tn=LANES)
    wa_f = jnp.pad(wa2_f, ((0, LANES - GLA_GATE_RANK), (0, 0)))
    wa_b = jnp.pad(wa2_b, ((GLA_GATE_RANK, LANES - lr), (0, 0)))
    y_conv = gated_conv(proj, conv_w, width)
    y_gla = gla_mixer(proj, a_lr, wa_f, ba_f.reshape(1, -1), wa_b, ba_b.reshape(1, -1),
                      head_norm.reshape(1, -1), col_q, col_k, col_v, col_g, dk, dv)
    w_out_b = w_out.astype(BF16)
    return matmul_residual([y_conv, y_gla], [w_out_b[:width], w_out_b[width:]], h)


def mla_layer(h, tables, norm, w_down, q_norm, kv_norm, w_uq, w_ukv, w_out):
    q_lora, kv_lora = q_norm.shape[0], kv_norm.shape[0]
    heads = w_uq.shape[1] // (MLA_NOPE + MLA_ROPE)
    n_down = w_down.shape[1]
    n_pad = q_lora + kv_lora + LANES
    w_down_b = jnp.pad(w_down, ((0, 0), (0, n_pad - n_down))).astype(BF16)
    down = norm_matmul(h, norm, w_down_b, F32, tm=256, tn=n_pad)
    tables, tables_t = tables
    cq, ckv, kr = mla_post(down, q_norm, kv_norm, tables, q_lora, kv_lora)
    w_uq_t = w_uq.reshape(q_lora, heads, MLA_NOPE + MLA_ROPE).transpose(1, 2, 0)
    w_uq_t = jnp.pad(w_uq_t, ((0, 0), (0, MLA_QK_PAD - MLA_NOPE - MLA_ROPE), (0, 0))).astype(BF16)
    w_ukv_h = w_ukv.reshape(kv_lora, heads, MLA_NOPE + MLA_V)
    w_uk_h = w_ukv_h[:, :, :MLA_NOPE].transpose(1, 0, 2).astype(BF16)
    w_uv_t = w_ukv_h[:, :, MLA_NOPE:].transpose(1, 2, 0).astype(BF16)
    qscale = (MLA_NOPE + MLA_ROPE) ** -0.5 * math.log2(math.e)
    qt = mla_q_up(cq, w_uq_t, tables_t, qscale)
    k, vt = mla_kv_up(ckv, w_uk_h, w_uv_t, kr)
    o = mla_attention(qt, k, vt)
    return matmul_residual([o], [w_out.astype(BF16)], h)


def kernel(x, positions, ab_norm, ab_w_in, ab_conv_w, gla_wa2_fwd, gla_ba_fwd, gla_wa2_bwd, gla_ba_bwd, gla_head_norm, ab_w_out, mla_norm, mla_w_down, mla_q_norm, mla_kv_norm, mla_w_uq, mla_w_ukv, mla_w_out, ffn_norm, router_w, expert_w_gate, expert_w_up, expert_w_down, final_norm):
    b_, s, d = x.shape
    depth = ffn_norm.shape[0]
    outs = []
    for b in range(b_):
        h = x[b]
        tables = (rope_tables(positions[b], 1), rope_tables(positions[b], 0))
        for i in range(depth):
            j = i // 2
            if i % 2 == 0:
                h = conv_gla_layer(h, ab_norm[j], ab_w_in[j], ab_conv_w[j], gla_wa2_fwd[j], gla_ba_fwd[j],
                                   gla_wa2_bwd[j], gla_ba_bwd[j], gla_head_norm[j], ab_w_out[j])
            else:
                h = mla_layer(h, tables, mla_norm[j], mla_w_down[j], mla_q_norm[j], mla_kv_norm[j],
                              mla_w_uq[j], mla_w_ukv[j], mla_w_out[j])
            h = expert_choice_ffn(h, ffn_norm[i], router_w[i].T, expert_w_gate[i].astype(BF16),
                                  expert_w_up[i].astype(BF16), expert_w_down[i].astype(BF16))
        outs.append(_final_norm_call(h, final_norm))
    return jnp.stack(outs)
```

```python
import functools
import math

import jax
import jax.numpy as jnp
from jax import lax
from jax.experimental import pallas as pl
from jax.experimental.pallas import tpu as pltpu

F32 = jnp.float32
BF16 = jnp.bfloat16
I32 = jnp.int32

EPS = 1e-6
LANES = 128
VMEM_LIMIT_BYTES = 56 * 1024 * 1024
GLA_HEADS = 8
GLA_GATE_RANK = 16
GLA_GATE_TAU = 16.0
GLA_CHUNK = 64
MLA_NOPE = 128
MLA_ROPE = 64
MLA_V = 128
MLA_QK_PAD = 256
ROPE_THETA = 10000.0
N_EXPERTS = 16
CAPACITY_FACTOR = 2

NT_DIMS = (((1,), (1,)), ((), ()))
TN_DIMS = (((0,), (0,)), ((), ()))


def _cparams(*sem):
    return pltpu.CompilerParams(dimension_semantics=sem, vmem_limit_bytes=VMEM_LIMIT_BYTES)


def _tile(n, pref):
    t = min(n, pref)
    assert n % t == 0, (n, pref)
    return t


def _dot(a, b):
    return jnp.dot(a, b, preferred_element_type=F32)


def _split3(a):
    p0 = a.astype(BF16)
    r0 = a - p0.astype(F32)
    p1 = r0.astype(BF16)
    p2 = (r0 - p1.astype(F32)).astype(BF16)
    return p0, p1, p2


def _dot_exact_lhs(a, b01):
    p0, p1, p2 = _split3(a)
    return (_dot(p0, b01) + _dot(p1, b01)) + _dot(p2, b01)


def _rms(x, gain):
    return x * lax.rsqrt(jnp.mean(x * x, axis=-1, keepdims=True) + EPS) * gain


def _norm_mm_kernel(h_ref, g_ref, w_ref, o_ref, xn_ref):
    @pl.when(pl.program_id(1) == 0)
    def _():
        xn_ref[...] = _rms(h_ref[...], g_ref[...]).astype(BF16)

    o_ref[...] = _dot(xn_ref[...], w_ref[...]).astype(o_ref.dtype)


def norm_matmul(h, gain, w, out_dtype, tm=512, tn=512, n=None):
    s, d = h.shape
    n = w.shape[1] if n is None else n
    tm, tn = _tile(s, tm), _tile(n, tn)
    return pl.pallas_call(
        _norm_mm_kernel,
        grid=(s // tm, n // tn),
        in_specs=[pl.BlockSpec((tm, d), lambda i, j: (i, 0)),
                  pl.BlockSpec((1, d), lambda i, j: (0, 0)),
                  pl.BlockSpec((d, tn), lambda i, j: (0, j))],
        out_specs=pl.BlockSpec((tm, tn), lambda i, j: (i, j)),
        out_shape=jax.ShapeDtypeStruct((s, n), out_dtype),
        scratch_shapes=[pltpu.VMEM((tm, d), BF16)],
        compiler_params=_cparams("parallel", "arbitrary"),
        name="norm_matmul",
    )(h, gain.reshape(1, d), w)


def _mm_res_kernel(*refs, n_in):
    a_refs, w_refs = refs[:n_in], refs[n_in:2 * n_in]
    res_ref, o_ref = refs[2 * n_in], refs[2 * n_in + 1]
    acc = res_ref[...]
    for a, w in zip(a_refs, w_refs):
        acc = acc + _dot(a[...], w[...])
    o_ref[...] = acc


def matmul_residual(a_list, w_list, res, tm=512, tn=1024):
    s, n = res.shape
    tm, tn = _tile(s, tm), _tile(n, tn)
    in_specs = [pl.BlockSpec((tm, a.shape[1]), lambda i, j: (i, 0)) for a in a_list]
    in_specs += [pl.BlockSpec((w.shape[0], tn), lambda i, j: (0, j)) for w in w_list]
    in_specs += [pl.BlockSpec((tm, tn), lambda i, j: (i, j))]
    return pl.pallas_call(
        functools.partial(_mm_res_kernel, n_in=len(a_list)),
        grid=(s // tm, n // tn),
        in_specs=in_specs,
        out_specs=pl.BlockSpec((tm, tn), lambda i, j: (i, j)),
        out_shape=jax.ShapeDtypeStruct((s, n), F32),
        compiler_params=_cparams("parallel", "arbitrary"),
        name="matmul_residual",
    )(*a_list, *w_list, res)


HALO = 16


def _conv_kernel(b_ref, c_ref, u_ref, cp_ref, up_ref, cn_ref, un_ref, w_ref, o_ref):
    i, nt = pl.program_id(0), pl.num_programs(0)
    tm = c_ref.shape[0]
    z = c_ref[...].astype(F32) * u_ref[...].astype(F32)
    zp = cp_ref[HALO - 1:HALO, :].astype(F32) * up_ref[HALO - 1:HALO, :].astype(F32)
    zn = cn_ref[0:1, :].astype(F32) * un_ref[0:1, :].astype(F32)
    zp = jnp.where(i > 0, zp, 0.0)
    zn = jnp.where(i < nt - 1, zn, 0.0)
    rows = lax.broadcasted_iota(I32, z.shape, 0)
    z_prev = jnp.where(rows == 0, zp, pltpu.roll(z, 1, axis=0))
    z_next = jnp.where(rows == tm - 1, zn, pltpu.roll(z, tm - 1, axis=0))
    w = w_ref[...]
    y = w[0:1] * z_prev + w[1:2] * z + w[2:3] * z_next
    o_ref[...] = (b_ref[...].astype(F32) * y).astype(o_ref.dtype)


def gated_conv(proj, conv_w, width, tm=512, tc=512):
    s = proj.shape[0]
    tm, tc = _tile(s, tm), _tile(width, tc)
    nc = width // tc
    nh = s // HALO
    per = tm // HALO

    def prev_map(off):
        return lambda i, c: (jnp.maximum(i * per - 1, 0), off * nc + c)

    def next_map(off):
        return lambda i, c: (jnp.minimum((i + 1) * per, nh - 1), off * nc + c)

    return pl.pallas_call(
        _conv_kernel,
        grid=(s // tm, nc),
        in_specs=[pl.BlockSpec((tm, tc), lambda i, c: (i, c)),
                  pl.BlockSpec((tm, tc), lambda i, c: (i, nc + c)),
                  pl.BlockSpec((tm, tc), lambda i, c: (i, 2 * nc + c)),
                  pl.BlockSpec((HALO, tc), prev_map(1)),
                  pl.BlockSpec((HALO, tc), prev_map(2)),
                  pl.BlockSpec((HALO, tc), next_map(1)),
                  pl.BlockSpec((HALO, tc), next_map(2)),
                  pl.BlockSpec((3, tc), lambda i, c: (0, c))],
        out_specs=pl.BlockSpec((tm, tc), lambda i, c: (i, c)),
        out_shape=jax.ShapeDtypeStruct((s, width), BF16),
        compiler_params=_cparams("parallel", "parallel"),
        name="gated_conv",
    )(proj, proj, proj, proj, proj, proj, proj, conv_w)


def _log_sigmoid(z):
    return -(jnp.maximum(-z, 0.0) + jnp.log1p(jnp.exp(-jnp.abs(z))))


def _gla_log_decay(a_ref, wa_ref, ba_ref):
    a = a_ref[...]
    w = wa_ref[...]
    a_hi = a.astype(BF16)
    a_lo = (a - a_hi.astype(F32)).astype(BF16)
    w_hi = w.astype(BF16)
    w_lo = (w - w_hi.astype(F32)).astype(BF16)
    z = (_dot(a_hi, w_hi) + _dot(a_lo, w_hi)) + _dot(a_hi, w_lo) + ba_ref[...]
    return _log_sigmoid(z) / GLA_GATE_TAU


def _bdot(spec, a, b):
    return jnp.einsum(spec, a, b, preferred_element_type=F32)


def _gla_tile(q, k, v, la, state_t, reverse):
    t, dk = q.shape
    dv = v.shape[1]
    n = GLA_CHUNK
    nc = t // n
    r = lax.broadcasted_iota(I32, (nc, n, n), 1)
    c = lax.broadcasted_iota(I32, (nc, n, n), 2)
    tri = jnp.where((c >= r) if reverse else (c <= r), 1.0, 0.0).astype(BF16)
    la3 = la.reshape(nc, n, dk)
    la_hi = la3.astype(BF16)
    la_lo = (la3 - la_hi.astype(F32)).astype(BF16)
    cum = _bdot('cij,cjd->cid', tri, la_hi) + _bdot('cij,cjd->cid', tri, la_lo)
    cum_last = cum[:, 0:1] if reverse else cum[:, n - 1:n]
    q3, k3, v3 = q.reshape(nc, n, dk), k.reshape(nc, n, dk), v.reshape(nc, n, dv)
    qd = (q3 * jnp.exp(cum)).astype(BF16)
    kd = (k3 * jnp.exp(-cum)).astype(BF16)
    ke = (k3 * jnp.exp(cum_last - cum)).astype(BF16)
    dec = jnp.exp(cum_last)
    sc = _bdot('cid,cjd->cij', qd, kd)
    sc = jnp.where((c > r) if reverse else (c <= r), sc, 0.0).astype(BF16)
    o_intra = _bdot('cij,cje->cie', sc, v3)
    kv = _bdot('cje,cjd->ced', v3, ke)
    incoming = [None] * nc
    state = state_t
    for ci in (range(nc - 1, -1, -1) if reverse else range(nc)):
        incoming[ci] = state.astype(BF16)
        state = state * dec[ci] + kv[ci]
    o_inter = _bdot('cid,ced->cie', qd, jnp.stack(incoming))
    return (o_intra + o_inter).reshape(t, dv), state


GLA_HEAD_GROUP = 2


def _gla_scan_tile(q_ref, k_ref, v_ref, la, st_ref, reverse, emit):
    group, dv, dk = st_ref.shape
    scale = dk ** -0.5
    for g in range(group):
        kc, vc = slice(g * dk, (g + 1) * dk), slice(g * dv, (g + 1) * dv)
        q = q_ref[:, kc].astype(F32) * scale
        k = k_ref[:, kc].astype(F32)
        o, st_ref[g] = _gla_tile(q, k, v_ref[:, vc], la[:, kc], st_ref[g], reverse)
        emit(vc, o)


def _gla_fwd_kernel(q_ref, k_ref, v_ref, a_ref, wa_ref, ba_ref, o_ref, st_ref):
    @pl.when(pl.program_id(1) == 0)
    def _():
        st_ref[...] = jnp.zeros_like(st_ref)

    la = _gla_log_decay(a_ref, wa_ref, ba_ref)

    def emit(vc, o):
        o_ref[:, vc] = o

    _gla_scan_tile(q_ref, k_ref, v_ref, la, st_ref, False, emit)


def _gla_bwd_kernel(q_ref, k_ref, v_ref, a_ref, wa_ref, ba_ref, of_ref, g_ref, hn_ref, y_ref, st_ref):
    @pl.when(pl.program_id(1) == 0)
    def _():
        st_ref[...] = jnp.zeros_like(st_ref)

    la = _gla_log_decay(a_ref, wa_ref, ba_ref)

    def emit(vc, o):
        tot = _rms(of_ref[:, vc] + o, hn_ref[...])
        g = g_ref[:, vc].astype(F32)
        y_ref[:, vc] = (tot * (g * jax.nn.sigmoid(g))).astype(y_ref.dtype)

    _gla_scan_tile(q_ref, k_ref, v_ref, la, st_ref, True, emit)


def gla_mixer(proj, a_lr, wa_f, ba_f, wa_b, ba_b, head_norm, col_q, col_k, col_v, col_g, dk, dv, tc=512):
    s = proj.shape[0]
    tc = _tile(s, tc)
    nt = s // tc
    grp = GLA_HEAD_GROUP
    ng = GLA_HEADS // grp
    gk, gv = grp * dk, grp * dv
    assert col_q % gk == 0 and col_k % gk == 0 and col_v % gv == 0 and col_g % gv == 0
    lr = a_lr.shape[1]

    def specs(row):
        return [pl.BlockSpec((tc, gk), lambda h, i: (row(i), col_q // gk + h)),
                pl.BlockSpec((tc, gk), lambda h, i: (row(i), col_k // gk + h)),
                pl.BlockSpec((tc, gv), lambda h, i: (row(i), col_v // gv + h)),
                pl.BlockSpec((tc, lr), lambda h, i: (row(i), 0)),
                pl.BlockSpec((lr, gk), lambda h, i: (0, h)),
                pl.BlockSpec((1, gk), lambda h, i: (0, h))]

    fwd_row = lambda i: i
    o_fwd = pl.pallas_call(
        _gla_fwd_kernel,
        grid=(ng, nt),
        in_specs=specs(fwd_row),
        out_specs=pl.BlockSpec((tc, gv), lambda h, i: (i, h)),
        out_shape=jax.ShapeDtypeStruct((s, GLA_HEADS * dv), F32),
        scratch_shapes=[pltpu.VMEM((grp, dv, dk), F32)],
        compiler_params=_cparams("parallel", "arbitrary"),
        name="gla_fwd",
    )(proj, proj, proj, a_lr, wa_f, ba_f)

    bwd_row = lambda i: nt - 1 - i
    return pl.pallas_call(
        _gla_bwd_kernel,
        grid=(ng, nt),
        in_specs=specs(bwd_row) + [
            pl.BlockSpec((tc, gv), lambda h, i: (bwd_row(i), h)),
            pl.BlockSpec((tc, gv), lambda h, i: (bwd_row(i), col_g // gv + h)),
            pl.BlockSpec((1, dv), lambda h, i: (0, 0))],
        out_specs=pl.BlockSpec((tc, gv), lambda h, i: (bwd_row(i), h)),
        out_shape=jax.ShapeDtypeStruct((s, GLA_HEADS * dv), BF16),
        scratch_shapes=[pltpu.VMEM((grp, dv, dk), F32)],
        compiler_params=_cparams("parallel", "arbitrary"),
        name="gla_bwd",
    )(proj, proj, proj, a_lr, wa_b, ba_b, o_fwd, proj, head_norm)


def _rope_tables_kernel(pos_ref, invf_ref, cos_ref, sina_ref, sinb_ref, *, axis):
    ang = pos_ref[...].astype(F32) * invf_ref[...]
    feat = lax.broadcasted_iota(I32, ang.shape, axis)
    half = MLA_ROPE // 2
    c, s_ = jnp.cos(ang), jnp.sin(ang)
    cos_ref[...] = jnp.where(feat < 2 * half, c, 0.0)
    sina_ref[...] = jnp.where(feat < half, -s_, 0.0)
    sinb_ref[...] = jnp.where((feat >= half) & (feat < 2 * half), s_, 0.0)


def rope_tables(positions, axis, tm=512):
    s = positions.shape[0]
    tm = _tile(s, tm)
    half = MLA_ROPE // 2
    inv_freq = ROPE_THETA ** (-jnp.arange(0, MLA_ROPE, 2, dtype=F32) / MLA_ROPE)
    invf = jnp.concatenate([inv_freq, inv_freq, jnp.zeros((LANES - 2 * half,), F32)])
    if axis == 1:
        pos, invf = positions.reshape(s, 1), invf.reshape(1, LANES)
        pos_spec, invf_spec = pl.BlockSpec((tm, 1), lambda i: (i, 0)), pl.BlockSpec((1, LANES), lambda i: (0, 0))
        spec, shape = pl.BlockSpec((tm, LANES), lambda i: (i, 0)), (s, LANES)
    else:
        pos, invf = positions.reshape(1, s), invf.reshape(LANES, 1)
        pos_spec, invf_spec = pl.BlockSpec((1, tm), lambda i: (0, i)), pl.BlockSpec((LANES, 1), lambda i: (0, 0))
        spec, shape = pl.BlockSpec((LANES, tm), lambda i: (0, i)), (LANES, s)
    return pl.pallas_call(
        functools.partial(_rope_tables_kernel, axis=axis),
        grid=(s // tm,),
        in_specs=[pos_spec, invf_spec],
        out_specs=[spec, spec, spec],
        out_shape=[jax.ShapeDtypeStruct(shape, F32)] * 3,
        compiler_params=_cparams("parallel"),
        name="rope_tables",
    )(pos, invf)


def _rope(x, cos_t, sin_a, sin_b, axis):
    half = MLA_ROPE // 2
    return x * cos_t + pltpu.roll(x, LANES - half, axis=axis) * sin_a + pltpu.roll(x, half, axis=axis) * sin_b


def _mla_post_kernel(d_ref, qn_ref, kvn_ref, cos_ref, sina_ref, sinb_ref, cq_ref, ckv_ref, kr_ref, *, q_lora, kv_lora):
    d = d_ref[...]
    cq_ref[...] = _rms(d[:, :q_lora], qn_ref[...]).astype(BF16)
    ckv_ref[...] = _rms(d[:, q_lora:q_lora + kv_lora], kvn_ref[...]).astype(BF16)
    kr = d[:, q_lora + kv_lora:]
    kr_ref[...] = _rope(kr, cos_ref[...], sina_ref[...], sinb_ref[...], 1).astype(BF16)


def mla_post(down, q_norm, kv_norm, tables, q_lora, kv_lora, tm=512):
    s, n = down.shape
    tm = _tile(s, tm)
    tab = pl.BlockSpec((tm, LANES), lambda i: (i, 0))
    return pl.pallas_call(
        functools.partial(_mla_post_kernel, q_lora=q_lora, kv_lora=kv_lora),
        grid=(s // tm,),
        in_specs=[pl.BlockSpec((tm, n), lambda i: (i, 0)),
                  pl.BlockSpec((1, q_lora), lambda i: (0, 0)),
                  pl.BlockSpec((1, kv_lora), lambda i: (0, 0)), tab, tab, tab],
        out_specs=[pl.BlockSpec((tm, q_lora), lambda i: (i, 0)),
                   pl.BlockSpec((tm, kv_lora), lambda i: (i, 0)),
                   pl.BlockSpec((tm, LANES), lambda i: (i, 0))],
        out_shape=[jax.ShapeDtypeStruct((s, q_lora), BF16),
                   jax.ShapeDtypeStruct((s, kv_lora), BF16),
                   jax.ShapeDtypeStruct((s, LANES), BF16)],
        compiler_params=_cparams("parallel"),
        name="mla_post",
    )(down, q_norm.reshape(1, -1), kv_norm.reshape(1, -1), *tables)


HEAD_GROUP = 4


def _q_up_kernel(x_ref, w_ref, cos_ref, sina_ref, sinb_ref, q_ref, *, qscale):
    x = x_ref[...]
    for g in range(w_ref.shape[0]):
        a = lax.dot_general(w_ref[g], x, NT_DIMS, preferred_element_type=F32)
        rot = _rope(a[MLA_NOPE:, :], cos_ref[...], sina_ref[...], sinb_ref[...], 0)
        q_ref[g] = (jnp.concatenate([a[:MLA_NOPE, :], rot], axis=0) * qscale).astype(BF16)


def mla_q_up(cq, w_uq_t, tables_t, qscale, tm=512):
    s, r = cq.shape
    h_ = w_uq_t.shape[0]
    tm = _tile(s, tm)
    g = _tile(h_, HEAD_GROUP)
    tab = pl.BlockSpec((LANES, tm), lambda i, h: (0, i))
    return pl.pallas_call(
        functools.partial(_q_up_kernel, qscale=qscale),
        grid=(s // tm, h_ // g),
        in_specs=[pl.BlockSpec((tm, r), lambda i, h: (i, 0)),
                  pl.BlockSpec((g, MLA_QK_PAD, r), lambda i, h: (h, 0, 0)), tab, tab, tab],
        out_specs=pl.BlockSpec((g, MLA_QK_PAD, tm), lambda i, h: (h, 0, i)),
        out_shape=jax.ShapeDtypeStruct((h_, MLA_QK_PAD, s), BF16),
        compiler_params=_cparams("parallel", "arbitrary"),
        name="mla_q_up",
    )(cq, w_uq_t, *tables_t)


def _kv_up_kernel(x_ref, wk_ref, wvt_ref, kr_ref, k_ref, vt_ref):
    x = x_ref[...]
    for g in range(wk_ref.shape[0]):
        k_ref[g] = jnp.concatenate([_dot(x, wk_ref[g]).astype(BF16), kr_ref[...]], axis=1)
        vt = lax.dot_general(wvt_ref[g], x, NT_DIMS, preferred_element_type=F32).astype(BF16)
        ones = jnp.ones((vt_ref.shape[2] - MLA_V, vt.shape[1]), BF16)
        vt_ref[g, 0] = jnp.concatenate([vt, ones], axis=0)


V_ONES_ROWS = 16
ATTN_UNROLL = 8


ATTN_KEY_BLOCK = 512


def mla_kv_up(ckv, w_uk_h, w_uv_t, kr, tm=ATTN_KEY_BLOCK):
    s, r = ckv.shape
    h_ = w_uk_h.shape[0]
    tm = _tile(s, tm)
    g = _tile(h_, HEAD_GROUP)
    return pl.pallas_call(
        _kv_up_kernel,
        grid=(s // tm, h_ // g),
        in_specs=[pl.BlockSpec((tm, r), lambda i, h: (i, 0)),
                  pl.BlockSpec((g, r, MLA_NOPE), lambda i, h: (h, 0, 0)),
                  pl.BlockSpec((g, MLA_V, r), lambda i, h: (h, 0, 0)),
                  pl.BlockSpec((tm, LANES), lambda i, h: (i, 0))],
        out_specs=[pl.BlockSpec((g, tm, MLA_QK_PAD), lambda i, h: (h, i, 0)),
                   pl.BlockSpec((g, 1, MLA_V + V_ONES_ROWS, tm), lambda i, h: (h, i, 0, 0))],
        out_shape=[jax.ShapeDtypeStruct((h_, s, MLA_QK_PAD), BF16),
                   jax.ShapeDtypeStruct((h_, s // tm, MLA_V + V_ONES_ROWS, tm), BF16)],
        compiler_params=_cparams("parallel", "arbitrary"),
        name="mla_kv_up",
    )(ckv, w_uk_h, w_uv_t, kr)


def _attn_kernel(qt_ref, k_ref, vt_ref, o_ref, s_ref, acc_ref, *, tkb):
    nkb = vt_ref.shape[1]
    tq = qt_ref.shape[2]
    qt = qt_ref[0]

    def scores(j, slot):
        off = pl.multiple_of(j * tkb, tkb)
        s = _dot(k_ref[0, pl.ds(off, tkb), :], qt)
        s_ref[slot] = s
        return jnp.max(s, axis=0, keepdims=True)

    def softmax_pv(j, slot, m, blk_max):
        m_new = jnp.maximum(m, blk_max)
        alpha = jnp.exp2(m - m_new)
        p = jnp.exp2((s_ref[slot] - m_new).astype(BF16))
        acc_ref[...] = alpha * acc_ref[...] + _dot(vt_ref[0, j], p)
        return m_new

    acc_ref[...] = jnp.zeros_like(acc_ref)
    m = jnp.full((1, tq), -jnp.inf, F32)
    unroll = min(ATTN_UNROLL, nkb)

    def trip(base, carry, last):
        m, blk_max = carry
        for u in range(unroll):
            nxt_max = blk_max if (last and u == unroll - 1) else scores(base + u + 1, (u + 1) % 2)
            m = softmax_pv(base + u, u % 2, m, blk_max)
            blk_max = nxt_max
        return m, blk_max

    carry = (m, scores(0, 0))
    carry = lax.fori_loop(0, nkb // unroll - 1, lambda i, c: trip(i * unroll, c, False), carry)
    trip(nkb - unroll, carry, True)
    dv = o_ref.shape[1]
    o_ref[...] = (acc_ref[:dv, :] / acc_ref[dv:dv + 1, :]).T.astype(o_ref.dtype)


def mla_attention(qt, k, vt, tq=512):
    h_, dq, s = qt.shape
    _, nkb, dve, tkb = vt.shape
    dv = dve - V_ONES_ROWS
    tq = _tile(s, tq)
    assert nkb % min(ATTN_UNROLL, nkb) == 0 and (min(ATTN_UNROLL, nkb) % 2 == 0 or nkb == 1)
    return pl.pallas_call(
        functools.partial(_attn_kernel, tkb=tkb),
        grid=(h_, s // tq),
        in_specs=[pl.BlockSpec((1, dq, tq), lambda h, i: (h, 0, i)),
                  pl.BlockSpec((1, s, dq), lambda h, i: (h, 0, 0)),
                  pl.BlockSpec((1, nkb, dve, tkb), lambda h, i: (h, 0, 0, 0))],
        out_specs=pl.BlockSpec((tq, dv), lambda h, i: (i, h)),
        out_shape=jax.ShapeDtypeStruct((s, h_ * dv), BF16),
        scratch_shapes=[pltpu.VMEM((2, tkb, tq), F32), pltpu.VMEM((dve, tq), F32)],
        compiler_params=_cparams("parallel", "arbitrary"),
        name="mla_attention",
    )(qt, k, vt)


def _router_kernel(h_ref, g_ref, wt_ref, aff_ref):
    xn = _rms(h_ref[...], g_ref[...])
    x0, x1, _ = _split3(xn)
    w0, w1, _ = _split3(wt_ref[...])

    def nt(a, b):
        return lax.dot_general(a, b, NT_DIMS, preferred_element_type=F32)

    logits = (nt(w1, x0) + nt(w0, x1)) + nt(w0, x0)
    e = jnp.exp(logits - jnp.max(logits, axis=0, keepdims=True))
    aff_ref[...] = e / jnp.sum(e, axis=0, keepdims=True)


def router_affinity(h, gain, w_router_t, tm=512):
    s, d = h.shape
    e_ = w_router_t.shape[0]
    tm = _tile(s, tm)
    return pl.pallas_call(
        _router_kernel,
        grid=(s // tm,),
        in_specs=[pl.BlockSpec((tm, d), lambda i: (i, 0)),
                  pl.BlockSpec((1, d), lambda i: (0, 0)),
                  pl.BlockSpec((e_, d), lambda i: (0, 0))],
        out_specs=pl.BlockSpec((e_, tm), lambda i: (0, i)),
        out_shape=jax.ShapeDtypeStruct((e_, s), F32),
        compiler_params=_cparams("parallel"),
        name="router_affinity",
    )(h, gain.reshape(1, d), w_router_t)


def _prefix_parts(xb, u_incl, ones_b, sl_strict):
    loc = _dot(xb, u_incl)
    tot = _dot(xb, ones_b)
    base = _dot(sl_strict, tot.astype(BF16))
    return loc, tot, base


def _select_kernel(aff_ref, idx_ref, gate_ref, rank_ref, start_ref, end_ref, sel_ref, *, cap):
    e_, nt, ln = aff_ref.shape
    li = lax.broadcasted_iota(I32, (ln, ln), 0)
    lj = lax.broadcasted_iota(I32, (ln, ln), 1)
    u_incl = jnp.where(li <= lj, 1.0, 0.0).astype(BF16)
    u_incl_t = jnp.where(lj <= li, 1.0, 0.0).astype(BF16)
    ones_b = jnp.ones((ln, ln), BF16)
    ti = lax.broadcasted_iota(I32, (nt, nt), 0)
    tj = lax.broadcasted_iota(I32, (nt, nt), 1)
    sl_strict = jnp.where(tj < ti, 1.0, 0.0).astype(BF16)

    bits = pltpu.bitcast(aff_ref[...], I32)

    def count_ge(thr):
        ge = jnp.where(bits >= thr, 1.0, 0.0)
        return jnp.sum(jnp.sum(ge, axis=2, keepdims=True), axis=1, keepdims=True)

    def search(b, thr):
        cand = thr | jnp.left_shift(jnp.int32(1), 30 - b)
        return jnp.where(count_ge(cand) >= cap, cand, thr)

    thr = lax.fori_loop(0, 31, search, jnp.zeros((e_, 1, 1), I32))
    gt = jnp.where(bits > thr, 1.0, 0.0)
    eq = jnp.where(bits == thr, 1.0, 0.0)
    need = cap - jnp.sum(jnp.sum(gt, axis=2, keepdims=True), axis=1, keepdims=True)

    cnt = jnp.zeros((nt, ln), F32)
    for e in range(e_):
        loc, _, base = _prefix_parts(eq[e].astype(BF16), u_incl, ones_b, sl_strict)
        eq_rank = loc + base - eq[e]
        sel = gt[e] + eq[e] * jnp.where(eq_rank < need[e], 1.0, 0.0)
        sel_ref[e] = sel
        cnt = cnt + sel

    cnt_b = cnt.astype(BF16)
    loc_c = _dot(cnt_b, u_incl)
    tot_c = _dot(cnt_b, ones_b)
    t0, t1, t2 = _split3(tot_c)
    base_c = (_dot(sl_strict, t0) + _dot(sl_strict, t1)) + _dot(sl_strict, t2)
    end = loc_c + base_c
    start = end - cnt
    start_ref[...] = start.astype(I32)
    end_ref[...] = end.astype(I32)

    p_row = lax.broadcasted_iota(I32, (1, cap), 1).astype(F32)
    tile_iota = lax.broadcasted_iota(I32, (nt, cap), 0).astype(F32)
    lane_iota = lax.broadcasted_iota(I32, (ln, cap), 0).astype(F32)

    def compact(e, within):
        sel = sel_ref[e]
        selb = sel.astype(BF16)
        _, tot, base = _prefix_parts(selb, u_incl, ones_b, sl_strict)
        tile_end = (base + tot)[:, 0:1]
        jp = jnp.sum(jnp.where(tile_end <= p_row, 1.0, 0.0), axis=0, keepdims=True)
        g_t = jnp.where(tile_iota == jp, 1.0, 0.0)
        base_p = jnp.sum(g_t * base[:, 0:1], axis=0, keepdims=True)
        r_p = p_row - base_p
        loc_t = lax.dot_general(u_incl_t, selb, NT_DIMS, preferred_element_type=F32)
        sel_loc = _dot(loc_t.astype(BF16), g_t.astype(BF16))
        lane_p = jnp.sum(jnp.where(sel_loc <= r_p, 1.0, 0.0), axis=0, keepdims=True)
        l_t = jnp.where(lane_iota == lane_p, 1.0, 0.0)
        l_b = l_t.astype(BF16)
        idx_ref[e] = (jp * ln + lane_p).astype(I32)
        gate_ref[e] = jnp.sum(g_t * _dot_exact_lhs(aff_ref[e], l_b), axis=0, keepdims=True)
        rank_full = start + within
        rank_ref[e] = jnp.sum(g_t * _dot_exact_lhs(rank_full, l_b), axis=0, keepdims=True).astype(I32)
        return within + sel

    lax.fori_loop(0, e_, compact, jnp.zeros((nt, ln), F32))


def expert_select(aff3, cap):
    e_, nt, ln = aff3.shape
    full3 = lambda a, b, c: pl.BlockSpec((a, b, c), lambda i: (0, 0, 0))
    full2 = lambda a, b: pl.BlockSpec((a, b), lambda i: (0, 0))
    return pl.pallas_call(
        functools.partial(_select_kernel, cap=cap),
        grid=(1,),
        in_specs=[full3(e_, nt, ln)],
        out_specs=[full3(e_, 1, cap), full3(e_, 1, cap), full3(e_, 1, cap), full2(nt, ln), full2(nt, ln)],
        out_shape=[jax.ShapeDtypeStruct((e_, 1, cap), I32),
                   jax.ShapeDtypeStruct((e_, 1, cap), F32),
                   jax.ShapeDtypeStruct((e_, 1, cap), I32),
                   jax.ShapeDtypeStruct((nt, ln), I32),
                   jax.ShapeDtypeStruct((nt, ln), I32)],
        scratch_shapes=[pltpu.VMEM((e_, nt, ln), F32)],
        compiler_params=_cparams("arbitrary"),
        name="expert_select",
    )(aff3)


DMA_ISSUE_UNROLL = 8


def _row_gather_copy(h_hbm, xbuf, sem, slot, tok, p):
    return pltpu.make_async_copy(h_hbm.at[pl.ds(tok, 1)], xbuf.at[slot, pl.ds(p, 1)], sem.at[slot])


def _ffn_up_kernel(idx_ref, h_hbm, g_ref, wg_ref, wu_ref, hid_ref, xbuf, sem):
    tm = xbuf.shape[1]
    step = pl.program_id(0) * pl.num_programs(1) + pl.program_id(1)
    nsteps = pl.num_programs(0) * pl.num_programs(1)
    slot = step % 2

    def issue(s_, slot_):
        def body(p, carry):
            _row_gather_copy(h_hbm, xbuf, sem, slot_, idx_ref[s_ * tm + p], p).start()
            return carry
        lax.fori_loop(0, tm, body, 0, unroll=DMA_ISSUE_UNROLL)

    @pl.when(step == 0)
    def _():
        issue(step, slot)

    @pl.when(step + 1 < nsteps)
    def _():
        issue(step + 1, 1 - slot)

    pltpu.make_async_copy(h_hbm.at[pl.ds(0, tm)], xbuf.at[slot], sem.at[slot]).wait()
    xn = _rms(xbuf[slot], g_ref[...]).astype(BF16)
    gate = _dot(xn, wg_ref[0])
    up = _dot(xn, wu_ref[0])
    hid_ref[...] = (gate * jax.nn.sigmoid(gate) * up).astype(hid_ref.dtype)


def ffn_up(h, gain, idx_flat, w_gate, w_up, cap, tm=512):
    s, d = h.shape
    e_, _, ff = w_gate.shape
    tm = _tile(cap, tm)
    r = cap // tm
    grid_spec = pltpu.PrefetchScalarGridSpec(
        num_scalar_prefetch=1,
        grid=(e_, r),
        in_specs=[pl.BlockSpec(memory_space=pl.ANY),
                  pl.BlockSpec((1, d), lambda e, i, idx: (0, 0)),
                  pl.BlockSpec((1, d, ff), lambda e, i, idx: (e, 0, 0)),
                  pl.BlockSpec((1, d, ff), lambda e, i, idx: (e, 0, 0))],
        out_specs=pl.BlockSpec((tm, ff), lambda e, i, idx: (e * r + i, 0)),
        scratch_shapes=[pltpu.VMEM((2, tm, d), F32), pltpu.SemaphoreType.DMA((2,))],
    )
    return pl.pallas_call(
        _ffn_up_kernel,
        grid_spec=grid_spec,
        out_shape=jax.ShapeDtypeStruct((e_ * cap, ff), BF16),
        compiler_params=_cparams("arbitrary", "arbitrary"),
        name="ffn_up",
    )(idx_flat, h, gain.reshape(1, d), w_gate, w_up)


def _pack_bf16_pairs(lo, hi):
    lo_bits = lax.bitcast_convert_type(lo.astype(BF16).astype(F32), jnp.uint32)
    hi_bits = lax.bitcast_convert_type(hi.astype(BF16).astype(F32), jnp.uint32)
    return (hi_bits & jnp.uint32(0xFFFF0000)) | (lo_bits >> 16)


def _unpack_bf16_pairs(w):
    lo = lax.bitcast_convert_type(w << 16, F32).astype(BF16)
    hi = lax.bitcast_convert_type(w & jnp.uint32(0xFFFF0000), F32).astype(BF16)
    return lo, hi


def _row_scatter_copy(ybuf, y_hbm, sem, slot, p, row):
    return pltpu.make_async_copy(ybuf.at[slot, pl.ds(p, 1)], y_hbm.at[pl.ds(row, 1)], sem.at[slot])


def _ffn_down_kernel(rank_ref, hid_ref, wd_ref, gate_ref, y_hbm, ybuf, sem):
    tm, half = ybuf.shape[1], ybuf.shape[2]
    step = pl.program_id(0) * pl.num_programs(1) + pl.program_id(1)
    nsteps = pl.num_programs(0) * pl.num_programs(1)
    slot = step % 2

    def drain(slot_):
        pltpu.make_async_copy(ybuf.at[slot_], y_hbm.at[pl.ds(0, tm)], sem.at[slot_]).wait()

    @pl.when(step >= 2)
    def _():
        drain(slot)

    y = _dot(hid_ref[...], wd_ref[0]) * gate_ref[...]
    ybuf[slot] = _pack_bf16_pairs(y[:, :half], y[:, half:])

    def body(p, carry):
        _row_scatter_copy(ybuf, y_hbm, sem, slot, p, rank_ref[step * tm + p]).start()
        return carry
    lax.fori_loop(0, tm, body, 0, unroll=DMA_ISSUE_UNROLL)

    @pl.when(step == nsteps - 1)
    def _():
        drain(slot)

        @pl.when(nsteps >= 2)
        def _():
            drain(1 - slot)


def ffn_down(hid, w_down, gate_col, rank_flat, cap, tm=512):
    n, ff = hid.shape
    e_, _, d = w_down.shape
    tm = _tile(cap, tm)
    r = cap // tm
    grid_spec = pltpu.PrefetchScalarGridSpec(
        num_scalar_prefetch=1,
        grid=(e_, r),
        in_specs=[pl.BlockSpec((tm, ff), lambda e, i, rk: (e * r + i, 0)),
                  pl.BlockSpec((1, ff, d), lambda e, i, rk: (e, 0, 0)),
                  pl.BlockSpec((tm, 1), lambda e, i, rk: (e * r + i, 0))],
        out_specs=pl.BlockSpec(memory_space=pl.ANY),
        scratch_shapes=[pltpu.VMEM((2, tm, d // 2), jnp.uint32), pltpu.SemaphoreType.DMA((2,))],
    )
    return pl.pallas_call(
        _ffn_down_kernel,
        grid_spec=grid_spec,
        out_shape=jax.ShapeDtypeStruct((n, d // 2), jnp.uint32),
        compiler_params=_cparams("arbitrary", "arbitrary"),
        name="ffn_down",
    )(rank_flat, hid, w_down, gate_col)


def _combine_kernel(ws_ref, vb_ref, h_ref, st_ref, en_ref, y_hbm, *rest, nchunks, out_norm):
    if out_norm:
        gain_ref, o_ref, win, sem = rest
    else:
        o_ref, win, sem = rest
    i, nt = pl.program_id(0), pl.num_programs(0)
    wrows, half = win.shape[1], win.shape[2]

    def first_chunk(t):
        return jnp.minimum(ws_ref[t] // wrows, nchunks - 1)

    def chunk_copy(chunk, slot):
        row0 = pl.multiple_of(chunk * wrows, wrows)
        return pltpu.make_async_copy(y_hbm.at[pl.ds(row0, wrows)], win.at[slot], sem.at[slot])

    lo_c = first_chunk(i)
    v0 = vb_ref[i]
    n_i = vb_ref[i + 1] - v0

    @pl.when(i == 0)
    def _():
        chunk_copy(lo_c, 0).start()

    st, en = st_ref[...], en_ref[...]
    o_ref[...] = h_ref[...]

    def visit(k, carry):
        slot = (v0 + k) % 2
        chunk = lo_c + k
        last_in_tile = k == n_i - 1
        nxt = jnp.where(last_in_tile, first_chunk(jnp.minimum(i + 1, nt - 1)), chunk + 1)

        @pl.when(jnp.logical_or(jnp.logical_not(last_in_tile), i + 1 < nt))
        def _():
            chunk_copy(nxt, 1 - slot).start()

        chunk_copy(chunk, slot).wait()
        r_abs = chunk * wrows + lax.broadcasted_iota(I32, (1, wrows), 1)
        q = jnp.where((st <= r_abs) & (r_abs < en), 1.0, 0.0).astype(BF16)
        lo, hi = _unpack_bf16_pairs(win[slot])
        o_ref[:, :half] += _dot(q, lo)
        o_ref[:, half:] += _dot(q, hi)
        return carry

    lax.fori_loop(0, n_i, visit, 0)
    if out_norm:
        o_ref[...] = _rms(o_ref[...], gain_ref[...])


def _combine_visits(win_starts, wrows, nchunks):
    lo_c = jnp.minimum(win_starts[:-1] // wrows, nchunks - 1)
    hi_c = jnp.maximum(lo_c, (win_starts[1:] + wrows - 1) // wrows - 1)
    return hi_c - lo_c + 1


def ffn_combine(h, y_sorted, start_col, end_col, win_starts, tm, out_gain=None, wrows=256):
    s, d = h.shape
    total = y_sorted.shape[0]
    wrows = _tile(total, wrows)
    nchunks = total // wrows
    n_visits = _combine_visits(win_starts, wrows, nchunks)
    visit_base = jnp.concatenate([jnp.zeros((1,), I32), jnp.cumsum(n_visits, dtype=I32)])
    in_specs = [pl.BlockSpec((tm, d), lambda i, ws, vb: (i, 0)),
                pl.BlockSpec((tm, 1), lambda i, ws, vb: (i, 0)),
                pl.BlockSpec((tm, 1), lambda i, ws, vb: (i, 0)),
                pl.BlockSpec(memory_space=pl.ANY)]
    operands = [win_starts, visit_base, h, start_col, end_col, y_sorted]
    if out_gain is not None:
        in_specs.append(pl.BlockSpec((1, d), lambda i, ws, vb: (0, 0)))
        operands.append(out_gain.reshape(1, d))
    grid_spec = pltpu.PrefetchScalarGridSpec(
        num_scalar_prefetch=2,
        grid=(s // tm,),
        in_specs=in_specs,
        out_specs=pl.BlockSpec((tm, d), lambda i, ws, vb: (i, 0)),
        scratch_shapes=[pltpu.VMEM((2, wrows, d // 2), jnp.uint32), pltpu.SemaphoreType.DMA((2,))],
    )
    return pl.pallas_call(
        functools.partial(_combine_kernel, nchunks=nchunks, out_norm=out_gain is not None),
        grid_spec=grid_spec,
        out_shape=jax.ShapeDtypeStruct((s, d), F32),
        compiler_params=_cparams("arbitrary"),
        name="ffn_combine",
    )(*operands)


def expert_choice_ffn(h, gain, w_router_t, w_gate, w_up, w_down, out_gain=None, combine_tm=256):
    s, d = h.shape
    e_ = w_router_t.shape[0]
    cap = CAPACITY_FACTOR * s // e_
    aff = router_affinity(h, gain, w_router_t)
    idx, gate, rank, start, end = expert_select(aff.reshape(e_, s // LANES, LANES), cap)
    hid = ffn_up(h, gain, idx.reshape(-1), w_gate, w_up, cap)
    y_sorted = ffn_down(hid, w_down, gate.reshape(-1, 1), rank.reshape(-1), cap)
    start_flat = start.reshape(-1)
    tm = _tile(s, combine_tm)
    win_starts = jnp.concatenate([start_flat[::tm], jnp.full((1,), e_ * cap, I32)])
    return ffn_combine(h, y_sorted, start_flat.reshape(s, 1), end.reshape(s, 1), win_starts, tm, out_gain)


def conv_gla_layer(h, norm, w_in, conv_w, wa2_f, ba_f, wa2_b, ba_b, head_norm, w_out):
    d = h.shape[1]
    width = conv_w.shape[1]
    kdim = wa2_f.shape[1]
    dk = kdim // GLA_HEADS
    dv = head_norm.shape[0]
    vdim = dv * GLA_HEADS
    n_main = 3 * width + 2 * kdim + 2 * vdim
    lr = 2 * GLA_GATE_RANK
    col_q = 3 * width
    col_k, col_v, col_g = col_q + kdim, col_q + 2 * kdim, col_q + 2 * kdim + vdim

    w_lr = jnp.pad(w_in[:, n_main:], ((0, 0), (0, LANES - lr))).astype(BF16)
    proj = norm_matmul(h, norm, w_in.astype(BF16), BF16, tn=1024, n=n_main)
    a_lr = norm_matmul(h, norm, w_lr, F32, tn=LANES)
    wa_f = jnp.pad(wa2_f, ((0, LANES - GLA_GATE_RANK), (0, 0)))
    wa_b = jnp.pad(wa2_b, ((GLA_GATE_RANK, LANES - lr), (0, 0)))
    y_conv = gated_conv(proj, conv_w, width)
    y_gla = gla_mixer(proj, a_lr, wa_f, ba_f.reshape(1, -1), wa_b, ba_b.reshape(1, -1),
                      head_norm.reshape(1, -1), col_q, col_k, col_v, col_g, dk, dv)
    w_out_b = w_out.astype(BF16)
    return matmul_residual([y_conv, y_gla], [w_out_b[:width], w_out_b[width:]], h)


def mla_layer(h, tables, norm, w_down, q_norm, kv_norm, w_uq, w_ukv, w_out):
    q_lora, kv_lora = q_norm.shape[0], kv_norm.shape[0]
    heads = w_uq.shape[1] // (MLA_NOPE + MLA_ROPE)
    n_down = w_down.shape[1]
    n_pad = q_lora + kv_lora + LANES
    w_down_b = jnp.pad(w_down, ((0, 0), (0, n_pad - n_down))).astype(BF16)
    down = norm_matmul(h, norm, w_down_b, F32, tm=256, tn=n_pad)
    tables, tables_t = tables
    cq, ckv, kr = mla_post(down, q_norm, kv_norm, tables, q_lora, kv_lora)
    w_uq_t = w_uq.reshape(q_lora, heads, MLA_NOPE + MLA_ROPE).transpose(1, 2, 0)
    w_uq_t = jnp.pad(w_uq_t, ((0, 0), (0, MLA_QK_PAD - MLA_NOPE - MLA_ROPE), (0, 0))).astype(BF16)
    w_ukv_h = w_ukv.reshape(kv_lora, heads, MLA_NOPE + MLA_V)
    w_uk_h = w_ukv_h[:, :, :MLA_NOPE].transpose(1, 0, 2).astype(BF16)
    w_uv_t = w_ukv_h[:, :, MLA_NOPE:].transpose(1, 2, 0).astype(BF16)
    qscale = (MLA_NOPE + MLA_ROPE) ** -0.5 * math.log2(math.e)
    qt = mla_q_up(cq, w_uq_t, tables_t, qscale)
    k, vt = mla_kv_up(ckv, w_uk_h, w_uv_t, kr)
    o = mla_attention(qt, k, vt)
    return matmul_residual([o], [w_out.astype(BF16)], h)


def kernel(x, positions, ab_norm, ab_w_in, ab_conv_w, gla_wa2_fwd, gla_ba_fwd, gla_wa2_bwd, gla_ba_bwd, gla_head_norm, ab_w_out, mla_norm, mla_w_down, mla_q_norm, mla_kv_norm, mla_w_uq, mla_w_ukv, mla_w_out, ffn_norm, router_w, expert_w_gate, expert_w_up, expert_w_down, final_norm):
    b_, s, d = x.shape
    depth = ffn_norm.shape[0]
    outs = []
    for b in range(b_):
        h = x[b]
        tables = (rope_tables(positions[b], 1), rope_tables(positions[b], 0))
        for i in range(depth):
            j = i // 2
            if i % 2 == 0:
                h = conv_gla_layer(h, ab_norm[j], ab_w_in[j], ab_conv_w[j], gla_wa2_fwd[j], gla_ba_fwd[j],
                                   gla_wa2_bwd[j], gla_ba_bwd[j], gla_head_norm[j], ab_w_out[j])
            else:
                h = mla_layer(h, tables, mla_norm[j], mla_w_down[j], mla_q_norm[j], mla_kv_norm[j],
                              mla_w_uq[j], mla_w_ukv[j], mla_w_out[j])
            h = expert_choice_ffn(h, ffn_norm[i], router_w[i].T, expert_w_gate[i].astype(BF16),
                                  expert_w_up[i].astype(BF16), expert_w_down[i].astype(BF16),
                                  out_gain=final_norm if i == depth - 1 else None)
        outs.append(h)
    return jnp.stack(outs)
```

```python
import functools
import math

import jax
import jax.numpy as jnp
from jax import lax
from jax.experimental import pallas as pl
from jax.experimental.pallas import tpu as pltpu

F32 = jnp.float32
BF16 = jnp.bfloat16
I32 = jnp.int32

EPS = 1e-6
LANES = 128
VMEM_LIMIT_BYTES = 56 * 1024 * 1024
GLA_HEADS = 8
GLA_GATE_RANK = 16
GLA_GATE_TAU = 16.0
GLA_CHUNK = 64
MLA_NOPE = 128
MLA_ROPE = 64
MLA_V = 128
MLA_QK_PAD = 256
ROPE_THETA = 10000.0
N_EXPERTS = 16
CAPACITY_FACTOR = 2

NT_DIMS = (((1,), (1,)), ((), ()))
TN_DIMS = (((0,), (0,)), ((), ()))


def _cparams(*sem):
    return pltpu.CompilerParams(dimension_semantics=sem, vmem_limit_bytes=VMEM_LIMIT_BYTES)


def _tile(n, pref):
    t = min(n, pref)
    assert n % t == 0, (n, pref)
    return t


def _dot(a, b):
    return jnp.dot(a, b, preferred_element_type=F32)


def _split3(a):
    p0 = a.astype(BF16)
    r0 = a - p0.astype(F32)
    p1 = r0.astype(BF16)
    p2 = (r0 - p1.astype(F32)).astype(BF16)
    return p0, p1, p2


def _dot_exact_lhs(a, b01):
    p0, p1, p2 = _split3(a)
    return (_dot(p0, b01) + _dot(p1, b01)) + _dot(p2, b01)


def _rms(x, gain):
    return x * lax.rsqrt(jnp.mean(x * x, axis=-1, keepdims=True) + EPS) * gain


def _norm_mm_kernel(h_ref, g_ref, w_ref, o_ref, xn_ref):
    @pl.when(pl.program_id(1) == 0)
    def _():
        xn_ref[...] = _rms(h_ref[...], g_ref[...]).astype(BF16)

    o_ref[...] = _dot(xn_ref[...], w_ref[...]).astype(o_ref.dtype)


def norm_matmul(h, gain, w, out_dtype, tm=512, tn=512, n=None):
    s, d = h.shape
    n = w.shape[1] if n is None else n
    tm, tn = _tile(s, tm), _tile(n, tn)
    return pl.pallas_call(
        _norm_mm_kernel,
        grid=(s // tm, n // tn),
        in_specs=[pl.BlockSpec((tm, d), lambda i, j: (i, 0)),
                  pl.BlockSpec((1, d), lambda i, j: (0, 0)),
                  pl.BlockSpec((d, tn), lambda i, j: (0, j))],
        out_specs=pl.BlockSpec((tm, tn), lambda i, j: (i, j)),
        out_shape=jax.ShapeDtypeStruct((s, n), out_dtype),
        scratch_shapes=[pltpu.VMEM((tm, d), BF16)],
        compiler_params=_cparams("parallel", "arbitrary"),
        name="norm_matmul",
    )(h, gain.reshape(1, d), w)


def _mm_res_kernel(*refs, n_in):
    a_refs, w_refs = refs[:n_in], refs[n_in:2 * n_in]
    res_ref, o_ref = refs[2 * n_in], refs[2 * n_in + 1]
    acc = res_ref[...]
    for a, w in zip(a_refs, w_refs):
        acc = acc + _dot(a[...], w[...])
    o_ref[...] = acc


def matmul_residual(a_list, w_list, res, tm=512, tn=1024):
    s, n = res.shape
    tm, tn = _tile(s, tm), _tile(n, tn)
    in_specs = [pl.BlockSpec((tm, a.shape[1]), lambda i, j: (i, 0)) for a in a_list]
    in_specs += [pl.BlockSpec((w.shape[0], tn), lambda i, j: (0, j)) for w in w_list]
    in_specs += [pl.BlockSpec((tm, tn), lambda i, j: (i, j))]
    return pl.pallas_call(
        functools.partial(_mm_res_kernel, n_in=len(a_list)),
        grid=(s // tm, n // tn),
        in_specs=in_specs,
        out_specs=pl.BlockSpec((tm, tn), lambda i, j: (i, j)),
        out_shape=jax.ShapeDtypeStruct((s, n), F32),
        compiler_params=_cparams("parallel", "arbitrary"),
        name="matmul_residual",
    )(*a_list, *w_list, res)


HALO = 16


def _conv_kernel(b_ref, c_ref, u_ref, cp_ref, up_ref, cn_ref, un_ref, w_ref, o_ref):
    i, nt = pl.program_id(0), pl.num_programs(0)
    tm = c_ref.shape[0]
    z = c_ref[...].astype(F32) * u_ref[...].astype(F32)
    zp = cp_ref[HALO - 1:HALO, :].astype(F32) * up_ref[HALO - 1:HALO, :].astype(F32)
    zn = cn_ref[0:1, :].astype(F32) * un_ref[0:1, :].astype(F32)
    zp = jnp.where(i > 0, zp, 0.0)
    zn = jnp.where(i < nt - 1, zn, 0.0)
    rows = lax.broadcasted_iota(I32, z.shape, 0)
    z_prev = jnp.where(rows == 0, zp, pltpu.roll(z, 1, axis=0))
    z_next = jnp.where(rows == tm - 1, zn, pltpu.roll(z, tm - 1, axis=0))
    w = w_ref[...]
    y = w[0:1] * z_prev + w[1:2] * z + w[2:3] * z_next
    o_ref[...] = (b_ref[...].astype(F32) * y).astype(o_ref.dtype)


def gated_conv(proj, conv_w, width, tm=512, tc=512):
    s = proj.shape[0]
    tm, tc = _tile(s, tm), _tile(width, tc)
    nc = width // tc
    nh = s // HALO
    per = tm // HALO

    def prev_map(off):
        return lambda i, c: (jnp.maximum(i * per - 1, 0), off * nc + c)

    def next_map(off):
        return lambda i, c: (jnp.minimum((i + 1) * per, nh - 1), off * nc + c)

    return pl.pallas_call(
        _conv_kernel,
        grid=(s // tm, nc),
        in_specs=[pl.BlockSpec((tm, tc), lambda i, c: (i, c)),
                  pl.BlockSpec((tm, tc), lambda i, c: (i, nc + c)),
                  pl.BlockSpec((tm, tc), lambda i, c: (i, 2 * nc + c)),
                  pl.BlockSpec((HALO, tc), prev_map(1)),
                  pl.BlockSpec((HALO, tc), prev_map(2)),
                  pl.BlockSpec((HALO, tc), next_map(1)),
                  pl.BlockSpec((HALO, tc), next_map(2)),
                  pl.BlockSpec((3, tc), lambda i, c: (0, c))],
        out_specs=pl.BlockSpec((tm, tc), lambda i, c: (i, c)),
        out_shape=jax.ShapeDtypeStruct((s, width), BF16),
        compiler_params=_cparams("parallel", "parallel"),
        name="gated_conv",
    )(proj, proj, proj, proj, proj, proj, proj, conv_w)


def _log_sigmoid(z):
    return -(jnp.maximum(-z, 0.0) + jnp.log1p(jnp.exp(-jnp.abs(z))))


def _gla_log_decay(a_ref, wa_ref, ba_ref):
    a = a_ref[...]
    w = wa_ref[...]
    a_hi = a.astype(BF16)
    a_lo = (a - a_hi.astype(F32)).astype(BF16)
    w_hi = w.astype(BF16)
    w_lo = (w - w_hi.astype(F32)).astype(BF16)
    z = (_dot(a_hi, w_hi) + _dot(a_lo, w_hi)) + _dot(a_hi, w_lo) + ba_ref[...]
    return _log_sigmoid(z) / GLA_GATE_TAU


def _bdot(spec, a, b):
    return jnp.einsum(spec, a, b, preferred_element_type=F32)


def _gla_tile(q, k, v, la, state_t, reverse):
    t, dk = q.shape
    dv = v.shape[1]
    n = GLA_CHUNK
    nc = t // n
    r = lax.broadcasted_iota(I32, (nc, n, n), 1)
    c = lax.broadcasted_iota(I32, (nc, n, n), 2)
    tri = jnp.where((c >= r) if reverse else (c <= r), 1.0, 0.0).astype(BF16)
    la3 = la.reshape(nc, n, dk)
    la_hi = la3.astype(BF16)
    la_lo = (la3 - la_hi.astype(F32)).astype(BF16)
    cum = _bdot('cij,cjd->cid', tri, la_hi) + _bdot('cij,cjd->cid', tri, la_lo)
    cum_last = cum[:, 0:1] if reverse else cum[:, n - 1:n]
    q3, k3, v3 = q.reshape(nc, n, dk), k.reshape(nc, n, dk), v.reshape(nc, n, dv)
    qd = (q3 * jnp.exp(cum)).astype(BF16)
    kd = (k3 * jnp.exp(-cum)).astype(BF16)
    ke = (k3 * jnp.exp(cum_last - cum)).astype(BF16)
    dec = jnp.exp(cum_last)
    sc = _bdot('cid,cjd->cij', qd, kd)
    sc = jnp.where((c > r) if reverse else (c <= r), sc, 0.0).astype(BF16)
    o_intra = _bdot('cij,cje->cie', sc, v3)
    kv = _bdot('cje,cjd->ced', v3, ke)
    incoming = [None] * nc
    state = state_t
    for ci in (range(nc - 1, -1, -1) if reverse else range(nc)):
        incoming[ci] = state.astype(BF16)
        state = state * dec[ci] + kv[ci]
    o_inter = _bdot('cid,ced->cie', qd, jnp.stack(incoming))
    return (o_intra + o_inter).reshape(t, dv), state


GLA_HEAD_GROUP = 4


def _gla_scan_tile(q_ref, k_ref, v_ref, la, st_ref, reverse, emit):
    group, dv, dk = st_ref.shape
    scale = dk ** -0.5
    for g in range(group):
        kc, vc = slice(g * dk, (g + 1) * dk), slice(g * dv, (g + 1) * dv)
        q = q_ref[:, kc].astype(F32) * scale
        k = k_ref[:, kc].astype(F32)
        o, st_ref[g] = _gla_tile(q, k, v_ref[:, vc], la[:, kc], st_ref[g], reverse)
        emit(vc, o)


def _gla_fwd_kernel(q_ref, k_ref, v_ref, a_ref, wa_ref, ba_ref, o_ref, st_ref):
    @pl.when(pl.program_id(1) == 0)
    def _():
        st_ref[...] = jnp.zeros_like(st_ref)

    la = _gla_log_decay(a_ref, wa_ref, ba_ref)

    def emit(vc, o):
        o_ref[:, vc] = o

    _gla_scan_tile(q_ref, k_ref, v_ref, la, st_ref, False, emit)


def _gla_bwd_kernel(q_ref, k_ref, v_ref, a_ref, wa_ref, ba_ref, of_ref, g_ref, hn_ref, y_ref, st_ref):
    @pl.when(pl.program_id(1) == 0)
    def _():
        st_ref[...] = jnp.zeros_like(st_ref)

    la = _gla_log_decay(a_ref, wa_ref, ba_ref)

    def emit(vc, o):
        tot = _rms(of_ref[:, vc] + o, hn_ref[...])
        g = g_ref[:, vc].astype(F32)
        y_ref[:, vc] = (tot * (g * jax.nn.sigmoid(g))).astype(y_ref.dtype)

    _gla_scan_tile(q_ref, k_ref, v_ref, la, st_ref, True, emit)


def gla_mixer(proj, a_lr, wa_f, ba_f, wa_b, ba_b, head_norm, col_q, col_k, col_v, col_g, dk, dv, tc=512):
    s = proj.shape[0]
    tc = _tile(s, tc)
    nt = s // tc
    grp = GLA_HEAD_GROUP
    ng = GLA_HEADS // grp
    gk, gv = grp * dk, grp * dv
    assert col_q % gk == 0 and col_k % gk == 0 and col_v % gv == 0 and col_g % gv == 0
    lr = a_lr.shape[1]

    def specs(row):
        return [pl.BlockSpec((tc, gk), lambda h, i: (row(i), col_q // gk + h)),
                pl.BlockSpec((tc, gk), lambda h, i: (row(i), col_k // gk + h)),
                pl.BlockSpec((tc, gv), lambda h, i: (row(i), col_v // gv + h)),
                pl.BlockSpec((tc, lr), lambda h, i: (row(i), 0)),
                pl.BlockSpec((lr, gk), lambda h, i: (0, h)),
                pl.BlockSpec((1, gk), lambda h, i: (0, h))]

    fwd_row = lambda i: i
    o_fwd = pl.pallas_call(
        _gla_fwd_kernel,
        grid=(ng, nt),
        in_specs=specs(fwd_row),
        out_specs=pl.BlockSpec((tc, gv), lambda h, i: (i, h)),
        out_shape=jax.ShapeDtypeStruct((s, GLA_HEADS * dv), F32),
        scratch_shapes=[pltpu.VMEM((grp, dv, dk), F32)],
        compiler_params=_cparams("parallel", "arbitrary"),
        name="gla_fwd",
    )(proj, proj, proj, a_lr, wa_f, ba_f)

    bwd_row = lambda i: nt - 1 - i
    return pl.pallas_call(
        _gla_bwd_kernel,
        grid=(ng, nt),
        in_specs=specs(bwd_row) + [
            pl.BlockSpec((tc, gv), lambda h, i: (bwd_row(i), h)),
            pl.BlockSpec((tc, gv), lambda h, i: (bwd_row(i), col_g // gv + h)),
            pl.BlockSpec((1, dv), lambda h, i: (0, 0))],
        out_specs=pl.BlockSpec((tc, gv), lambda h, i: (bwd_row(i), h)),
        out_shape=jax.ShapeDtypeStruct((s, GLA_HEADS * dv), BF16),
        scratch_shapes=[pltpu.VMEM((grp, dv, dk), F32)],
        compiler_params=_cparams("parallel", "arbitrary"),
        name="gla_bwd",
    )(proj, proj, proj, a_lr, wa_b, ba_b, o_fwd, proj, head_norm)


def _rope_tables_kernel(pos_ref, invf_ref, cos_ref, sina_ref, sinb_ref, *, axis):
    ang = pos_ref[...].astype(F32) * invf_ref[...]
    feat = lax.broadcasted_iota(I32, ang.shape, axis)
    half = MLA_ROPE // 2
    c, s_ = jnp.cos(ang), jnp.sin(ang)
    cos_ref[...] = jnp.where(feat < 2 * half, c, 0.0)
    sina_ref[...] = jnp.where(feat < half, -s_, 0.0)
    sinb_ref[...] = jnp.where((feat >= half) & (feat < 2 * half), s_, 0.0)


def rope_tables(positions, axis, tm=512):
    s = positions.shape[0]
    tm = _tile(s, tm)
    half = MLA_ROPE // 2
    inv_freq = ROPE_THETA ** (-jnp.arange(0, MLA_ROPE, 2, dtype=F32) / MLA_ROPE)
    invf = jnp.concatenate([inv_freq, inv_freq, jnp.zeros((LANES - 2 * half,), F32)])
    if axis == 1:
        pos, invf = positions.reshape(s, 1), invf.reshape(1, LANES)
        pos_spec, invf_spec = pl.BlockSpec((tm, 1), lambda i: (i, 0)), pl.BlockSpec((1, LANES), lambda i: (0, 0))
        spec, shape = pl.BlockSpec((tm, LANES), lambda i: (i, 0)), (s, LANES)
    else:
        pos, invf = positions.reshape(1, s), invf.reshape(LANES, 1)
        pos_spec, invf_spec = pl.BlockSpec((1, tm), lambda i: (0, i)), pl.BlockSpec((LANES, 1), lambda i: (0, 0))
        spec, shape = pl.BlockSpec((LANES, tm), lambda i: (0, i)), (LANES, s)
    return pl.pallas_call(
        functools.partial(_rope_tables_kernel, axis=axis),
        grid=(s // tm,),
        in_specs=[pos_spec, invf_spec],
        out_specs=[spec, spec, spec],
        out_shape=[jax.ShapeDtypeStruct(shape, F32)] * 3,
        compiler_params=_cparams("parallel"),
        name="rope_tables",
    )(pos, invf)


def _rope(x, cos_t, sin_a, sin_b, axis):
    half = MLA_ROPE // 2
    return x * cos_t + pltpu.roll(x, LANES - half, axis=axis) * sin_a + pltpu.roll(x, half, axis=axis) * sin_b


def _mla_post_kernel(d_ref, qn_ref, kvn_ref, cos_ref, sina_ref, sinb_ref, cq_ref, ckv_ref, kr_ref, *, q_lora, kv_lora):
    d = d_ref[...]
    cq_ref[...] = _rms(d[:, :q_lora], qn_ref[...]).astype(BF16)
    ckv_ref[...] = _rms(d[:, q_lora:q_lora + kv_lora], kvn_ref[...]).astype(BF16)
    kr = d[:, q_lora + kv_lora:]
    kr_ref[...] = _rope(kr, cos_ref[...], sina_ref[...], sinb_ref[...], 1).astype(BF16)


def mla_post(down, q_norm, kv_norm, tables, q_lora, kv_lora, tm=512):
    s, n = down.shape
    tm = _tile(s, tm)
    tab = pl.BlockSpec((tm, LANES), lambda i: (i, 0))
    return pl.pallas_call(
        functools.partial(_mla_post_kernel, q_lora=q_lora, kv_lora=kv_lora),
        grid=(s // tm,),
        in_specs=[pl.BlockSpec((tm, n), lambda i: (i, 0)),
                  pl.BlockSpec((1, q_lora), lambda i: (0, 0)),
                  pl.BlockSpec((1, kv_lora), lambda i: (0, 0)), tab, tab, tab],
        out_specs=[pl.BlockSpec((tm, q_lora), lambda i: (i, 0)),
                   pl.BlockSpec((tm, kv_lora), lambda i: (i, 0)),
                   pl.BlockSpec((tm, LANES), lambda i: (i, 0))],
        out_shape=[jax.ShapeDtypeStruct((s, q_lora), BF16),
                   jax.ShapeDtypeStruct((s, kv_lora), BF16),
                   jax.ShapeDtypeStruct((s, LANES), BF16)],
        compiler_params=_cparams("parallel"),
        name="mla_post",
    )(down, q_norm.reshape(1, -1), kv_norm.reshape(1, -1), *tables)


HEAD_GROUP = 4


def _q_up_kernel(x_ref, w_ref, cos_ref, sina_ref, sinb_ref, q_ref, *, qscale):
    x = x_ref[...]
    for g in range(w_ref.shape[0]):
        a = lax.dot_general(w_ref[g], x, NT_DIMS, preferred_element_type=F32)
        rot = _rope(a[MLA_NOPE:, :], cos_ref[...], sina_ref[...], sinb_ref[...], 0)
        q_ref[g] = (jnp.concatenate([a[:MLA_NOPE, :], rot], axis=0) * qscale).astype(BF16)


def mla_q_up(cq, w_uq_t, tables_t, qscale, tm=512):
    s, r = cq.shape
    h_ = w_uq_t.shape[0]
    tm = _tile(s, tm)
    g = _tile(h_, HEAD_GROUP)
    tab = pl.BlockSpec((LANES, tm), lambda i, h: (0, i))
    return pl.pallas_call(
        functools.partial(_q_up_kernel, qscale=qscale),
        grid=(s // tm, h_ // g),
        in_specs=[pl.BlockSpec((tm, r), lambda i, h: (i, 0)),
                  pl.BlockSpec((g, MLA_QK_PAD, r), lambda i, h: (h, 0, 0)), tab, tab, tab],
        out_specs=pl.BlockSpec((g, MLA_QK_PAD, tm), lambda i, h: (h, 0, i)),
        out_shape=jax.ShapeDtypeStruct((h_, MLA_QK_PAD, s), BF16),
        compiler_params=_cparams("parallel", "arbitrary"),
        name="mla_q_up",
    )(cq, w_uq_t, *tables_t)


def _kv_up_kernel(x_ref, wk_ref, wvt_ref, kr_ref, k_ref, vt_ref):
    x = x_ref[...]
    for g in range(wk_ref.shape[0]):
        k_ref[g] = jnp.concatenate([_dot(x, wk_ref[g]).astype(BF16), kr_ref[...]], axis=1)
        vt = lax.dot_general(wvt_ref[g], x, NT_DIMS, preferred_element_type=F32).astype(BF16)
        ones = jnp.ones((vt_ref.shape[2] - MLA_V, vt.shape[1]), BF16)
        vt_ref[g, 0] = jnp.concatenate([vt, ones], axis=0)


V_ONES_ROWS = 16
ATTN_UNROLL = 8


ATTN_KEY_BLOCK = 512


def mla_kv_up(ckv, w_uk_h, w_uv_t, kr, tm=ATTN_KEY_BLOCK):
    s, r = ckv.shape
    h_ = w_uk_h.shape[0]
    tm = _tile(s, tm)
    g = _tile(h_, HEAD_GROUP)
    return pl.pallas_call(
        _kv_up_kernel,
        grid=(s // tm, h_ // g),
        in_specs=[pl.BlockSpec((tm, r), lambda i, h: (i, 0)),
                  pl.BlockSpec((g, r, MLA_NOPE), lambda i, h: (h, 0, 0)),
                  pl.BlockSpec((g, MLA_V, r), lambda i, h: (h, 0, 0)),
                  pl.BlockSpec((tm, LANES), lambda i, h: (i, 0))],
        out_specs=[pl.BlockSpec((g, tm, MLA_QK_PAD), lambda i, h: (h, i, 0)),
                   pl.BlockSpec((g, 1, MLA_V + V_ONES_ROWS, tm), lambda i, h: (h, i, 0, 0))],
        out_shape=[jax.ShapeDtypeStruct((h_, s, MLA_QK_PAD), BF16),
                   jax.ShapeDtypeStruct((h_, s // tm, MLA_V + V_ONES_ROWS, tm), BF16)],
        compiler_params=_cparams("parallel", "arbitrary"),
        name="mla_kv_up",
    )(ckv, w_uk_h, w_uv_t, kr)


def _attn_kernel(qt_ref, k_ref, vt_ref, o_ref, s_ref, acc_ref, *, tkb):
    nkb = vt_ref.shape[1]
    tq = qt_ref.shape[2]
    qt = qt_ref[0]

    def scores(j, slot):
        off = pl.multiple_of(j * tkb, tkb)
        s = _dot(k_ref[0, pl.ds(off, tkb), :], qt)
        s_ref[slot] = s
        return jnp.max(s, axis=0, keepdims=True)

    def softmax_pv(j, slot, m, blk_max):
        m_new = jnp.maximum(m, blk_max)
        alpha = jnp.exp2(m - m_new)
        p = jnp.exp2((s_ref[slot] - m_new).astype(BF16))
        acc_ref[...] = alpha * acc_ref[...] + _dot(vt_ref[0, j], p)
        return m_new

    acc_ref[...] = jnp.zeros_like(acc_ref)
    m = jnp.full((1, tq), -jnp.inf, F32)
    unroll = min(ATTN_UNROLL, nkb)

    def trip(base, carry, last):
        m, blk_max = carry
        for u in range(unroll):
            nxt_max = blk_max if (last and u == unroll - 1) else scores(base + u + 1, (u + 1) % 2)
            m = softmax_pv(base + u, u % 2, m, blk_max)
            blk_max = nxt_max
        return m, blk_max

    carry = (m, scores(0, 0))
    carry = lax.fori_loop(0, nkb // unroll - 1, lambda i, c: trip(i * unroll, c, False), carry)
    trip(nkb - unroll, carry, True)
    dv = o_ref.shape[1]
    o_ref[...] = (acc_ref[:dv, :] / acc_ref[dv:dv + 1, :]).T.astype(o_ref.dtype)


def mla_attention(qt, k, vt, tq=512):
    h_, dq, s = qt.shape
    _, nkb, dve, tkb = vt.shape
    dv = dve - V_ONES_ROWS
    tq = _tile(s, tq)
    assert nkb % min(ATTN_UNROLL, nkb) == 0 and (min(ATTN_UNROLL, nkb) % 2 == 0 or nkb == 1)
    return pl.pallas_call(
        functools.partial(_attn_kernel, tkb=tkb),
        grid=(h_, s // tq),
        in_specs=[pl.BlockSpec((1, dq, tq), lambda h, i: (h, 0, i)),
                  pl.BlockSpec((1, s, dq), lambda h, i: (h, 0, 0)),
                  pl.BlockSpec((1, nkb, dve, tkb), lambda h, i: (h, 0, 0, 0))],
        out_specs=pl.BlockSpec((tq, dv), lambda h, i: (i, h)),
        out_shape=jax.ShapeDtypeStruct((s, h_ * dv), BF16),
        scratch_shapes=[pltpu.VMEM((2, tkb, tq), F32), pltpu.VMEM((dve, tq), F32)],
        compiler_params=_cparams("parallel", "arbitrary"),
        name="mla_attention",
    )(qt, k, vt)


def _router_kernel(h_ref, g_ref, wt_ref, aff_ref):
    xn = _rms(h_ref[...], g_ref[...])
    x0, x1, _ = _split3(xn)
    w0, w1, _ = _split3(wt_ref[...])

    def nt(a, b):
        return lax.dot_general(a, b, NT_DIMS, preferred_element_type=F32)

    logits = (nt(w1, x0) + nt(w0, x1)) + nt(w0, x0)
    e = jnp.exp(logits - jnp.max(logits, axis=0, keepdims=True))
    aff_ref[...] = e / jnp.sum(e, axis=0, keepdims=True)


def router_affinity(h, gain, w_router_t, tm=512):
    s, d = h.shape
    e_ = w_router_t.shape[0]
    tm = _tile(s, tm)
    return pl.pallas_call(
        _router_kernel,
        grid=(s // tm,),
        in_specs=[pl.BlockSpec((tm, d), lambda i: (i, 0)),
                  pl.BlockSpec((1, d), lambda i: (0, 0)),
                  pl.BlockSpec((e_, d), lambda i: (0, 0))],
        out_specs=pl.BlockSpec((e_, tm), lambda i: (0, i)),
        out_shape=jax.ShapeDtypeStruct((e_, s), F32),
        compiler_params=_cparams("parallel"),
        name="router_affinity",
    )(h, gain.reshape(1, d), w_router_t)


def _prefix_parts(xb, u_incl, ones_b, sl_strict):
    loc = _dot(xb, u_incl)
    tot = _dot(xb, ones_b)
    base = _dot(sl_strict, tot.astype(BF16))
    return loc, tot, base


def _select_kernel(aff_ref, idx_ref, gate_ref, rank_ref, start_ref, end_ref, sel_ref, *, cap):
    e_, nt, ln = aff_ref.shape
    li = lax.broadcasted_iota(I32, (ln, ln), 0)
    lj = lax.broadcasted_iota(I32, (ln, ln), 1)
    u_incl = jnp.where(li <= lj, 1.0, 0.0).astype(BF16)
    u_incl_t = jnp.where(lj <= li, 1.0, 0.0).astype(BF16)
    ones_b = jnp.ones((ln, ln), BF16)
    ti = lax.broadcasted_iota(I32, (nt, nt), 0)
    tj = lax.broadcasted_iota(I32, (nt, nt), 1)
    sl_strict = jnp.where(tj < ti, 1.0, 0.0).astype(BF16)

    bits = pltpu.bitcast(aff_ref[...], I32)

    def count_ge(thr):
        ge = jnp.where(bits >= thr, 1.0, 0.0)
        return jnp.sum(jnp.sum(ge, axis=2, keepdims=True), axis=1, keepdims=True)

    def search(b, thr):
        cand = thr | jnp.left_shift(jnp.int32(1), 30 - b)
        return jnp.where(count_ge(cand) >= cap, cand, thr)

    thr = lax.fori_loop(0, 31, search, jnp.zeros((e_, 1, 1), I32))
    gt = jnp.where(bits > thr, 1.0, 0.0)
    eq = jnp.where(bits == thr, 1.0, 0.0)
    need = cap - jnp.sum(jnp.sum(gt, axis=2, keepdims=True), axis=1, keepdims=True)

    cnt = jnp.zeros((nt, ln), F32)
    for e in range(e_):
        loc, _, base = _prefix_parts(eq[e].astype(BF16), u_incl, ones_b, sl_strict)
        eq_rank = loc + base - eq[e]
        sel = gt[e] + eq[e] * jnp.where(eq_rank < need[e], 1.0, 0.0)
        sel_ref[e] = sel
        cnt = cnt + sel

    cnt_b = cnt.astype(BF16)
    loc_c = _dot(cnt_b, u_incl)
    tot_c = _dot(cnt_b, ones_b)
    t0, t1, t2 = _split3(tot_c)
    base_c = (_dot(sl_strict, t0) + _dot(sl_strict, t1)) + _dot(sl_strict, t2)
    end = loc_c + base_c
    start = end - cnt
    start_ref[...] = start.astype(I32)
    end_ref[...] = end.astype(I32)

    p_row = lax.broadcasted_iota(I32, (1, cap), 1).astype(F32)
    tile_iota = lax.broadcasted_iota(I32, (nt, cap), 0).astype(F32)
    lane_iota = lax.broadcasted_iota(I32, (ln, cap), 0).astype(F32)

    def compact(e, within):
        sel = sel_ref[e]
        selb = sel.astype(BF16)
        _, tot, base = _prefix_parts(selb, u_incl, ones_b, sl_strict)
        tile_end = (base + tot)[:, 0:1]
        jp = jnp.sum(jnp.where(tile_end <= p_row, 1.0, 0.0), axis=0, keepdims=True)
        g_t = jnp.where(tile_iota == jp, 1.0, 0.0)
        base_p = jnp.sum(g_t * base[:, 0:1], axis=0, keepdims=True)
        r_p = p_row - base_p
        loc_t = lax.dot_general(u_incl_t, selb, NT_DIMS, preferred_element_type=F32)
        sel_loc = _dot(loc_t.astype(BF16), g_t.astype(BF16))
        lane_p = jnp.sum(jnp.where(sel_loc <= r_p, 1.0, 0.0), axis=0, keepdims=True)
        l_t = jnp.where(lane_iota == lane_p, 1.0, 0.0)
        l_b = l_t.astype(BF16)
        idx_ref[e] = (jp * ln + lane_p).astype(I32)
        gate_ref[e] = jnp.sum(g_t * _dot_exact_lhs(aff_ref[e], l_b), axis=0, keepdims=True)
        rank_full = start + within
        rank_ref[e] = jnp.sum(g_t * _dot_exact_lhs(rank_full, l_b), axis=0, keepdims=True).astype(I32)
        return within + sel

    lax.fori_loop(0, e_, compact, jnp.zeros((nt, ln), F32))


def expert_select(aff3, cap):
    e_, nt, ln = aff3.shape
    full3 = lambda a, b, c: pl.BlockSpec((a, b, c), lambda i: (0, 0, 0))
    full2 = lambda a, b: pl.BlockSpec((a, b), lambda i: (0, 0))
    return pl.pallas_call(
        functools.partial(_select_kernel, cap=cap),
        grid=(1,),
        in_specs=[full3(e_, nt, ln)],
        out_specs=[full3(e_, 1, cap), full3(e_, 1, cap), full3(e_, 1, cap), full2(nt, ln), full2(nt, ln)],
        out_shape=[jax.ShapeDtypeStruct((e_, 1, cap), I32),
                   jax.ShapeDtypeStruct((e_, 1, cap), F32),
                   jax.ShapeDtypeStruct((e_, 1, cap), I32),
                   jax.ShapeDtypeStruct((nt, ln), I32),
                   jax.ShapeDtypeStruct((nt, ln), I32)],
        scratch_shapes=[pltpu.VMEM((e_, nt, ln), F32)],
        compiler_params=_cparams("arbitrary"),
        name="expert_select",
    )(aff3)


DMA_ISSUE_UNROLL = 8


def _row_gather_copy(h_hbm, xbuf, sem, slot, tok, p):
    return pltpu.make_async_copy(h_hbm.at[pl.ds(tok, 1)], xbuf.at[slot, pl.ds(p, 1)], sem.at[slot])


def _ffn_up_kernel(idx_ref, h_hbm, g_ref, wg_ref, wu_ref, hid_ref, xbuf, sem):
    tm = xbuf.shape[1]
    step = pl.program_id(0) * pl.num_programs(1) + pl.program_id(1)
    nsteps = pl.num_programs(0) * pl.num_programs(1)
    slot = step % 2

    def issue(s_, slot_):
        def body(p, carry):
            _row_gather_copy(h_hbm, xbuf, sem, slot_, idx_ref[s_ * tm + p], p).start()
            return carry
        lax.fori_loop(0, tm, body, 0, unroll=DMA_ISSUE_UNROLL)

    @pl.when(step == 0)
    def _():
        issue(step, slot)

    @pl.when(step + 1 < nsteps)
    def _():
        issue(step + 1, 1 - slot)

    pltpu.make_async_copy(h_hbm.at[pl.ds(0, tm)], xbuf.at[slot], sem.at[slot]).wait()
    xn = _rms(xbuf[slot], g_ref[...]).astype(BF16)
    gate = _dot(xn, wg_ref[0])
    up = _dot(xn, wu_ref[0])
    hid_ref[...] = (gate * jax.nn.sigmoid(gate) * up).astype(hid_ref.dtype)


def ffn_up(h, gain, idx_flat, w_gate, w_up, w_base, e_, cap, tm=512):
    s, d = h.shape
    _, _, ff = w_gate.shape
    tm = _tile(cap, tm)
    r = cap // tm
    grid_spec = pltpu.PrefetchScalarGridSpec(
        num_scalar_prefetch=1,
        grid=(e_, r),
        in_specs=[pl.BlockSpec(memory_space=pl.ANY),
                  pl.BlockSpec((1, d), lambda e, i, idx: (0, 0)),
                  pl.BlockSpec((1, d, ff), lambda e, i, idx: (w_base + e, 0, 0)),
                  pl.BlockSpec((1, d, ff), lambda e, i, idx: (w_base + e, 0, 0))],
        out_specs=pl.BlockSpec((tm, ff), lambda e, i, idx: (e * r + i, 0)),
        scratch_shapes=[pltpu.VMEM((2, tm, d), F32), pltpu.SemaphoreType.DMA((2,))],
    )
    return pl.pallas_call(
        _ffn_up_kernel,
        grid_spec=grid_spec,
        out_shape=jax.ShapeDtypeStruct((e_ * cap, ff), BF16),
        compiler_params=_cparams("arbitrary", "arbitrary"),
        name="ffn_up",
    )(idx_flat, h, gain.reshape(1, d), w_gate, w_up)


def _pack_bf16_pairs(lo, hi):
    lo_bits = lax.bitcast_convert_type(lo.astype(BF16).astype(F32), jnp.uint32)
    hi_bits = lax.bitcast_convert_type(hi.astype(BF16).astype(F32), jnp.uint32)
    return (hi_bits & jnp.uint32(0xFFFF0000)) | (lo_bits >> 16)


def _unpack_bf16_pairs(w):
    lo = lax.bitcast_convert_type(w << 16, F32).astype(BF16)
    hi = lax.bitcast_convert_type(w & jnp.uint32(0xFFFF0000), F32).astype(BF16)
    return lo, hi


def _row_scatter_copy(ybuf, y_hbm, sem, slot, p, row):
    return pltpu.make_async_copy(ybuf.at[slot, pl.ds(p, 1)], y_hbm.at[pl.ds(row, 1)], sem.at[slot])


def _ffn_down_kernel(rank_ref, hid_ref, wd_ref, gate_ref, y_hbm, ybuf, sem):
    tm, half = ybuf.shape[1], ybuf.shape[2]
    step = pl.program_id(0) * pl.num_programs(1) + pl.program_id(1)
    nsteps = pl.num_programs(0) * pl.num_programs(1)
    slot = step % 2

    def drain(slot_):
        pltpu.make_async_copy(ybuf.at[slot_], y_hbm.at[pl.ds(0, tm)], sem.at[slot_]).wait()

    @pl.when(step >= 2)
    def _():
        drain(slot)

    y = _dot(hid_ref[...], wd_ref[0]) * gate_ref[...]
    ybuf[slot] = _pack_bf16_pairs(y[:, :half], y[:, half:])

    def body(p, carry):
        _row_scatter_copy(ybuf, y_hbm, sem, slot, p, rank_ref[step * tm + p]).start()
        return carry
    lax.fori_loop(0, tm, body, 0, unroll=DMA_ISSUE_UNROLL)

    @pl.when(step == nsteps - 1)
    def _():
        drain(slot)

        @pl.when(nsteps >= 2)
        def _():
            drain(1 - slot)


def ffn_down(hid, w_down, w_base, e_, gate_col, rank_flat, cap, tm=512):
    n, ff = hid.shape
    _, _, d = w_down.shape
    tm = _tile(cap, tm)
    r = cap // tm
    grid_spec = pltpu.PrefetchScalarGridSpec(
        num_scalar_prefetch=1,
        grid=(e_, r),
        in_specs=[pl.BlockSpec((tm, ff), lambda e, i, rk: (e * r + i, 0)),
                  pl.BlockSpec((1, ff, d), lambda e, i, rk: (w_base + e, 0, 0)),
                  pl.BlockSpec((tm, 1), lambda e, i, rk: (e * r + i, 0))],
        out_specs=pl.BlockSpec(memory_space=pl.ANY),
        scratch_shapes=[pltpu.VMEM((2, tm, d // 2), jnp.uint32), pltpu.SemaphoreType.DMA((2,))],
    )
    return pl.pallas_call(
        _ffn_down_kernel,
        grid_spec=grid_spec,
        out_shape=jax.ShapeDtypeStruct((n, d // 2), jnp.uint32),
        compiler_params=_cparams("arbitrary", "arbitrary"),
        name="ffn_down",
    )(rank_flat, hid, w_down, gate_col)


def _combine_kernel(ws_ref, vb_ref, h_ref, st_ref, en_ref, y_hbm, *rest, nchunks, out_norm):
    if out_norm:
        gain_ref, o_ref, win, sem = rest
    else:
        o_ref, win, sem = rest
    i, nt = pl.program_id(0), pl.num_programs(0)
    wrows, half = win.shape[1], win.shape[2]

    def first_chunk(t):
        return jnp.minimum(ws_ref[t] // wrows, nchunks - 1)

    def chunk_copy(chunk, slot):
        row0 = pl.multiple_of(chunk * wrows, wrows)
        return pltpu.make_async_copy(y_hbm.at[pl.ds(row0, wrows)], win.at[slot], sem.at[slot])

    lo_c = first_chunk(i)
    v0 = vb_ref[i]
    n_i = vb_ref[i + 1] - v0

    @pl.when(i == 0)
    def _():
        chunk_copy(lo_c, 0).start()

    st, en = st_ref[...], en_ref[...]
    o_ref[...] = h_ref[...]

    def visit(k, carry):
        slot = (v0 + k) % 2
        chunk = lo_c + k
        last_in_tile = k == n_i - 1
        nxt = jnp.where(last_in_tile, first_chunk(jnp.minimum(i + 1, nt - 1)), chunk + 1)

        @pl.when(jnp.logical_or(jnp.logical_not(last_in_tile), i + 1 < nt))
        def _():
            chunk_copy(nxt, 1 - slot).start()

        chunk_copy(chunk, slot).wait()
        r_abs = chunk * wrows + lax.broadcasted_iota(I32, (1, wrows), 1)
        q = jnp.where((st <= r_abs) & (r_abs < en), 1.0, 0.0).astype(BF16)
        lo, hi = _unpack_bf16_pairs(win[slot])
        o_ref[:, :half] += _dot(q, lo)
        o_ref[:, half:] += _dot(q, hi)
        return carry

    lax.fori_loop(0, n_i, visit, 0)
    if out_norm:
        o_ref[...] = _rms(o_ref[...], gain_ref[...])


def _combine_visits(win_starts, wrows, nchunks):
    lo_c = jnp.minimum(win_starts[:-1] // wrows, nchunks - 1)
    hi_c = jnp.maximum(lo_c, (win_starts[1:] + wrows - 1) // wrows - 1)
    return hi_c - lo_c + 1


def ffn_combine(h, y_sorted, start_col, end_col, win_starts, tm, out_gain=None, wrows=256):
    s, d = h.shape
    total = y_sorted.shape[0]
    wrows = _tile(total, wrows)
    nchunks = total // wrows
    n_visits = _combine_visits(win_starts, wrows, nchunks)
    visit_base = jnp.concatenate([jnp.zeros((1,), I32), jnp.cumsum(n_visits, dtype=I32)])
    in_specs = [pl.BlockSpec((tm, d), lambda i, ws, vb: (i, 0)),
                pl.BlockSpec((tm, 1), lambda i, ws, vb: (i, 0)),
                pl.BlockSpec((tm, 1), lambda i, ws, vb: (i, 0)),
                pl.BlockSpec(memory_space=pl.ANY)]
    operands = [win_starts, visit_base, h, start_col, end_col, y_sorted]
    if out_gain is not None:
        in_specs.append(pl.BlockSpec((1, d), lambda i, ws, vb: (0, 0)))
        operands.append(out_gain.reshape(1, d))
    grid_spec = pltpu.PrefetchScalarGridSpec(
        num_scalar_prefetch=2,
        grid=(s // tm,),
        in_specs=in_specs,
        out_specs=pl.BlockSpec((tm, d), lambda i, ws, vb: (i, 0)),
        scratch_shapes=[pltpu.VMEM((2, wrows, d // 2), jnp.uint32), pltpu.SemaphoreType.DMA((2,))],
    )
    return pl.pallas_call(
        functools.partial(_combine_kernel, nchunks=nchunks, out_norm=out_gain is not None),
        grid_spec=grid_spec,
        out_shape=jax.ShapeDtypeStruct((s, d), F32),
        compiler_params=_cparams("arbitrary"),
        name="ffn_combine",
    )(*operands)


def expert_choice_ffn(h, gain, w_router_t, w_gate, w_up, w_down, w_base, out_gain=None, combine_tm=256):
    s, d = h.shape
    e_ = w_router_t.shape[0]
    cap = CAPACITY_FACTOR * s // e_
    aff = router_affinity(h, gain, w_router_t)
    idx, gate, rank, start, end = expert_select(aff.reshape(e_, s // LANES, LANES), cap)
    hid = ffn_up(h, gain, idx.reshape(-1), w_gate, w_up, w_base, e_, cap)
    y_sorted = ffn_down(hid, w_down, w_base, e_, gate.reshape(-1, 1), rank.reshape(-1), cap)
    start_flat = start.reshape(-1)
    tm = _tile(s, combine_tm)
    win_starts = jnp.concatenate([start_flat[::tm], jnp.full((1,), e_ * cap, I32)])
    return ffn_combine(h, y_sorted, start_flat.reshape(s, 1), end.reshape(s, 1), win_starts, tm, out_gain)


def conv_gla_layer(h, norm, w_in, conv_w, wa2_f, ba_f, wa2_b, ba_b, head_norm, w_out):
    d = h.shape[1]
    width = conv_w.shape[1]
    kdim = wa2_f.shape[1]
    dk = kdim // GLA_HEADS
    dv = head_norm.shape[0]
    vdim = dv * GLA_HEADS
    n_main = 3 * width + 2 * kdim + 2 * vdim
    lr = 2 * GLA_GATE_RANK
    col_q = 3 * width
    col_k, col_v, col_g = col_q + kdim, col_q + 2 * kdim, col_q + 2 * kdim + vdim

    w_lr = jnp.pad(w_in[:, n_main:], ((0, 0), (0, LANES - lr))).astype(BF16)
    proj = norm_matmul(h, norm, w_in.astype(BF16), BF16, tn=1024, n=n_main)
    a_lr = norm_matmul(h, norm, w_lr, F32, tn=LANES)
    wa_f = jnp.pad(wa2_f, ((0, LANES - GLA_GATE_RANK), (0, 0)))
    wa_b = jnp.pad(wa2_b, ((GLA_GATE_RANK, LANES - lr), (0, 0)))
    y_conv = gated_conv(proj, conv_w, width)
    y_gla = gla_mixer(proj, a_lr, wa_f, ba_f.reshape(1, -1), wa_b, ba_b.reshape(1, -1),
                      head_norm.reshape(1, -1), col_q, col_k, col_v, col_g, dk, dv)
    w_out_b = w_out.astype(BF16)
    return matmul_residual([y_conv, y_gla], [w_out_b[:width], w_out_b[width:]], h)


def mla_layer(h, tables, norm, w_down, q_norm, kv_norm, w_uq, w_ukv, w_out):
    q_lora, kv_lora = q_norm.shape[0], kv_norm.shape[0]
    heads = w_uq.shape[1] // (MLA_NOPE + MLA_ROPE)
    n_down = w_down.shape[1]
    n_pad = q_lora + kv_lora + LANES
    w_down_b = jnp.pad(w_down, ((0, 0), (0, n_pad - n_down))).astype(BF16)
    down = norm_matmul(h, norm, w_down_b, F32, tm=256, tn=n_pad)
    tables, tables_t = tables
    cq, ckv, kr = mla_post(down, q_norm, kv_norm, tables, q_lora, kv_lora)
    w_uq_t = w_uq.reshape(q_lora, heads, MLA_NOPE + MLA_ROPE).transpose(1, 2, 0)
    w_uq_t = jnp.pad(w_uq_t, ((0, 0), (0, MLA_QK_PAD - MLA_NOPE - MLA_ROPE), (0, 0))).astype(BF16)
    w_ukv_h = w_ukv.reshape(kv_lora, heads, MLA_NOPE + MLA_V)
    w_uk_h = w_ukv_h[:, :, :MLA_NOPE].transpose(1, 0, 2).astype(BF16)
    w_uv_t = w_ukv_h[:, :, MLA_NOPE:].transpose(1, 2, 0).astype(BF16)
    qscale = (MLA_NOPE + MLA_ROPE) ** -0.5 * math.log2(math.e)
    qt = mla_q_up(cq, w_uq_t, tables_t, qscale)
    k, vt = mla_kv_up(ckv, w_uk_h, w_uv_t, kr)
    o = mla_attention(qt, k, vt)
    return matmul_residual([o], [w_out.astype(BF16)], h)


def kernel(x, positions, ab_norm, ab_w_in, ab_conv_w, gla_wa2_fwd, gla_ba_fwd, gla_wa2_bwd, gla_ba_bwd, gla_head_norm, ab_w_out, mla_norm, mla_w_down, mla_q_norm, mla_kv_norm, mla_w_uq, mla_w_ukv, mla_w_out, ffn_norm, router_w, expert_w_gate, expert_w_up, expert_w_down, final_norm):
    b_, s, d = x.shape
    depth = ffn_norm.shape[0]
    n_exp = expert_w_gate.shape[1]
    w_gate_all = expert_w_gate.astype(BF16).reshape((depth * n_exp,) + expert_w_gate.shape[2:])
    w_up_all = expert_w_up.astype(BF16).reshape((depth * n_exp,) + expert_w_up.shape[2:])
    w_down_all = expert_w_down.astype(BF16).reshape((depth * n_exp,) + expert_w_down.shape[2:])
    outs = []
    for b in range(b_):
        h = x[b]
        tables = (rope_tables(positions[b], 1), rope_tables(positions[b], 0))
        for i in range(depth):
            j = i // 2
            if i % 2 == 0:
                h = conv_gla_layer(h, ab_norm[j], ab_w_in[j], ab_conv_w[j], gla_wa2_fwd[j], gla_ba_fwd[j],
                                   gla_wa2_bwd[j], gla_ba_bwd[j], gla_head_norm[j], ab_w_out[j])
            else:
                h = mla_layer(h, tables, mla_norm[j], mla_w_down[j], mla_q_norm[j], mla_kv_norm[j],
                              mla_w_uq[j], mla_w_ukv[j], mla_w_out[j])
            h = expert_choice_ffn(h, ffn_norm[i], router_w[i].T, w_gate_all, w_up_all, w_down_all, i * n_exp,
                                  out_gain=final_norm if i == depth - 1 else None)
        outs.append(h)
    return jnp.stack(outs)
```

```python
import functools
import math

import jax
import jax.numpy as jnp
from jax import lax
from jax.experimental import pallas as pl
from jax.experimental.pallas import tpu as pltpu

F32 = jnp.float32
BF16 = jnp.bfloat16
I32 = jnp.int32

EPS = 1e-6
LANES = 128
VMEM_LIMIT_BYTES = 56 * 1024 * 1024
GLA_HEADS = 8
GLA_GATE_RANK = 16
GLA_GATE_TAU = 16.0
GLA_CHUNK = 64
MLA_NOPE = 128
MLA_ROPE = 64
MLA_V = 128
MLA_QK_PAD = 256
ROPE_THETA = 10000.0
N_EXPERTS = 16
CAPACITY_FACTOR = 2

NT_DIMS = (((1,), (1,)), ((), ()))
TN_DIMS = (((0,), (0,)), ((), ()))


def _cparams(*sem):
    return pltpu.CompilerParams(dimension_semantics=sem, vmem_limit_bytes=VMEM_LIMIT_BYTES)


def _tile(n, pref):
    t = min(n, pref)
    assert n % t == 0, (n, pref)
    return t


def _dot(a, b):
    return jnp.dot(a, b, preferred_element_type=F32)


def _split3(a):
    p0 = a.astype(BF16)
    r0 = a - p0.astype(F32)
    p1 = r0.astype(BF16)
    p2 = (r0 - p1.astype(F32)).astype(BF16)
    return p0, p1, p2


def _dot_exact_lhs(a, b01):
    p0, p1, p2 = _split3(a)
    return (_dot(p0, b01) + _dot(p1, b01)) + _dot(p2, b01)


def _rms(x, gain):
    return x * lax.rsqrt(jnp.mean(x * x, axis=-1, keepdims=True) + EPS) * gain


def _norm_mm_kernel(h_ref, g_ref, w_ref, o_ref, xn_ref):
    @pl.when(pl.program_id(1) == 0)
    def _():
        xn_ref[...] = _rms(h_ref[...], g_ref[...]).astype(BF16)

    o_ref[...] = _dot(xn_ref[...], w_ref[...]).astype(o_ref.dtype)


def norm_matmul(h, gain, w, out_dtype, tm=512, tn=512, n=None):
    s, d = h.shape
    n = w.shape[1] if n is None else n
    tm, tn = _tile(s, tm), _tile(n, tn)
    return pl.pallas_call(
        _norm_mm_kernel,
        grid=(s // tm, n // tn),
        in_specs=[pl.BlockSpec((tm, d), lambda i, j: (i, 0)),
                  pl.BlockSpec((1, d), lambda i, j: (0, 0)),
                  pl.BlockSpec((d, tn), lambda i, j: (0, j))],
        out_specs=pl.BlockSpec((tm, tn), lambda i, j: (i, j)),
        out_shape=jax.ShapeDtypeStruct((s, n), out_dtype),
        scratch_shapes=[pltpu.VMEM((tm, d), BF16)],
        compiler_params=_cparams("parallel", "arbitrary"),
        name="norm_matmul",
    )(h, gain.reshape(1, d), w)


def _mm_res_kernel(*refs, n_in):
    a_refs, w_refs = refs[:n_in], refs[n_in:2 * n_in]
    res_ref, o_ref = refs[2 * n_in], refs[2 * n_in + 1]
    acc = res_ref[...]
    for a, w in zip(a_refs, w_refs):
        acc = acc + _dot(a[...], w[...])
    o_ref[...] = acc


def matmul_residual(a_list, w_list, res, tm=1024, tn=1024):
    s, n = res.shape
    tm, tn = _tile(s, tm), _tile(n, tn)
    in_specs = [pl.BlockSpec((tm, a.shape[1]), lambda i, j: (i, 0)) for a in a_list]
    in_specs += [pl.BlockSpec((w.shape[0], tn), lambda i, j: (0, j)) for w in w_list]
    in_specs += [pl.BlockSpec((tm, tn), lambda i, j: (i, j))]
    return pl.pallas_call(
        functools.partial(_mm_res_kernel, n_in=len(a_list)),
        grid=(s // tm, n // tn),
        in_specs=in_specs,
        out_specs=pl.BlockSpec((tm, tn), lambda i, j: (i, j)),
        out_shape=jax.ShapeDtypeStruct((s, n), F32),
        compiler_params=_cparams("parallel", "arbitrary"),
        name="matmul_residual",
    )(*a_list, *w_list, res)


HALO = 16


def _conv_kernel(b_ref, c_ref, u_ref, cp_ref, up_ref, cn_ref, un_ref, w_ref, o_ref):
    i, nt = pl.program_id(0), pl.num_programs(0)
    tm = c_ref.shape[0]
    z = c_ref[...].astype(F32) * u_ref[...].astype(F32)
    zp = cp_ref[HALO - 1:HALO, :].astype(F32) * up_ref[HALO - 1:HALO, :].astype(F32)
    zn = cn_ref[0:1, :].astype(F32) * un_ref[0:1, :].astype(F32)
    zp = jnp.where(i > 0, zp, 0.0)
    zn = jnp.where(i < nt - 1, zn, 0.0)
    rows = lax.broadcasted_iota(I32, z.shape, 0)
    z_prev = jnp.where(rows == 0, zp, pltpu.roll(z, 1, axis=0))
    z_next = jnp.where(rows == tm - 1, zn, pltpu.roll(z, tm - 1, axis=0))
    w = w_ref[...]
    y = w[0:1] * z_prev + w[1:2] * z + w[2:3] * z_next
    o_ref[...] = (b_ref[...].astype(F32) * y).astype(o_ref.dtype)


def gated_conv(proj, conv_w, width, tm=512, tc=512):
    s = proj.shape[0]
    tm, tc = _tile(s, tm), _tile(width, tc)
    nc = width // tc
    nh = s // HALO
    per = tm // HALO

    def prev_map(off):
        return lambda i, c: (jnp.maximum(i * per - 1, 0), off * nc + c)

    def next_map(off):
        return lambda i, c: (jnp.minimum((i + 1) * per, nh - 1), off * nc + c)

    return pl.pallas_call(
        _conv_kernel,
        grid=(s // tm, nc),
        in_specs=[pl.BlockSpec((tm, tc), lambda i, c: (i, c)),
                  pl.BlockSpec((tm, tc), lambda i, c: (i, nc + c)),
                  pl.BlockSpec((tm, tc), lambda i, c: (i, 2 * nc + c)),
                  pl.BlockSpec((HALO, tc), prev_map(1)),
                  pl.BlockSpec((HALO, tc), prev_map(2)),
                  pl.BlockSpec((HALO, tc), next_map(1)),
                  pl.BlockSpec((HALO, tc), next_map(2)),
                  pl.BlockSpec((3, tc), lambda i, c: (0, c))],
        out_specs=pl.BlockSpec((tm, tc), lambda i, c: (i, c)),
        out_shape=jax.ShapeDtypeStruct((s, width), BF16),
        compiler_params=_cparams("parallel", "parallel"),
        name="gated_conv",
    )(proj, proj, proj, proj, proj, proj, proj, conv_w)


def _log_sigmoid(z):
    return -(jnp.maximum(-z, 0.0) + jnp.log1p(jnp.exp(-jnp.abs(z))))


def _gla_log_decay(a_ref, wa_ref, ba_ref):
    a = a_ref[...]
    w = wa_ref[...]
    a_hi = a.astype(BF16)
    a_lo = (a - a_hi.astype(F32)).astype(BF16)
    w_hi = w.astype(BF16)
    w_lo = (w - w_hi.astype(F32)).astype(BF16)
    z = (_dot(a_hi, w_hi) + _dot(a_lo, w_hi)) + _dot(a_hi, w_lo) + ba_ref[...]
    return _log_sigmoid(z) / GLA_GATE_TAU


def _bdot(spec, a, b):
    return jnp.einsum(spec, a, b, preferred_element_type=F32)


def _gla_tile(q, k, v, la, state_t, reverse):
    t, dk = q.shape
    dv = v.shape[1]
    n = GLA_CHUNK
    nc = t // n
    r = lax.broadcasted_iota(I32, (nc, n, n), 1)
    c = lax.broadcasted_iota(I32, (nc, n, n), 2)
    tri = jnp.where((c >= r) if reverse else (c <= r), 1.0, 0.0).astype(BF16)
    la3 = la.reshape(nc, n, dk)
    la_hi = la3.astype(BF16)
    la_lo = (la3 - la_hi.astype(F32)).astype(BF16)
    cum = _bdot('cij,cjd->cid', tri, la_hi) + _bdot('cij,cjd->cid', tri, la_lo)
    cum_last = cum[:, 0:1] if reverse else cum[:, n - 1:n]
    q3, k3, v3 = q.reshape(nc, n, dk), k.reshape(nc, n, dk), v.reshape(nc, n, dv)
    qd = (q3 * jnp.exp(cum)).astype(BF16)
    kd = (k3 * jnp.exp(-cum)).astype(BF16)
    ke = (k3 * jnp.exp(cum_last - cum)).astype(BF16)
    dec = jnp.exp(cum_last)
    sc = _bdot('cid,cjd->cij', qd, kd)
    sc = jnp.where((c > r) if reverse else (c <= r), sc, 0.0).astype(BF16)
    o_intra = _bdot('cij,cje->cie', sc, v3)
    kv = _bdot('cje,cjd->ced', v3, ke)
    incoming = [None] * nc
    state = state_t
    for ci in (range(nc - 1, -1, -1) if reverse else range(nc)):
        incoming[ci] = state.astype(BF16)
        state = state * dec[ci] + kv[ci]
    o_inter = _bdot('cid,ced->cie', qd, jnp.stack(incoming))
    return (o_intra + o_inter).reshape(t, dv), state


GLA_HEAD_GROUP = 4


def _gla_scan_tile(q_ref, k_ref, v_ref, la, st_ref, reverse, emit):
    group, dv, dk = st_ref.shape
    scale = dk ** -0.5
    for g in range(group):
        kc, vc = slice(g * dk, (g + 1) * dk), slice(g * dv, (g + 1) * dv)
        q = q_ref[:, kc].astype(F32) * scale
        k = k_ref[:, kc].astype(F32)
        o, st_ref[g] = _gla_tile(q, k, v_ref[:, vc], la[:, kc], st_ref[g], reverse)
        emit(vc, o)


def _gla_fwd_kernel(q_ref, k_ref, v_ref, a_ref, wa_ref, ba_ref, o_ref, st_ref):
    @pl.when(pl.program_id(1) == 0)
    def _():
        st_ref[...] = jnp.zeros_like(st_ref)

    la = _gla_log_decay(a_ref, wa_ref, ba_ref)

    def emit(vc, o):
        o_ref[:, vc] = o

    _gla_scan_tile(q_ref, k_ref, v_ref, la, st_ref, False, emit)


def _gla_bwd_kernel(q_ref, k_ref, v_ref, a_ref, wa_ref, ba_ref, of_ref, g_ref, hn_ref, y_ref, st_ref):
    @pl.when(pl.program_id(1) == 0)
    def _():
        st_ref[...] = jnp.zeros_like(st_ref)

    la = _gla_log_decay(a_ref, wa_ref, ba_ref)

    def emit(vc, o):
        tot = _rms(of_ref[:, vc] + o, hn_ref[...])
        g = g_ref[:, vc].astype(F32)
        y_ref[:, vc] = (tot * (g * jax.nn.sigmoid(g))).astype(y_ref.dtype)

    _gla_scan_tile(q_ref, k_ref, v_ref, la, st_ref, True, emit)


def gla_mixer(proj, a_lr, wa_f, ba_f, wa_b, ba_b, head_norm, col_q, col_k, col_v, col_g, dk, dv, tc=512):
    s = proj.shape[0]
    tc = _tile(s, tc)
    nt = s // tc
    grp = GLA_HEAD_GROUP
    ng = GLA_HEADS // grp
    gk, gv = grp * dk, grp * dv
    assert col_q % gk == 0 and col_k % gk == 0 and col_v % gv == 0 and col_g % gv == 0
    lr = a_lr.shape[1]

    def specs(row):
        return [pl.BlockSpec((tc, gk), lambda h, i: (row(i), col_q // gk + h)),
                pl.BlockSpec((tc, gk), lambda h, i: (row(i), col_k // gk + h)),
                pl.BlockSpec((tc, gv), lambda h, i: (row(i), col_v // gv + h)),
                pl.BlockSpec((tc, lr), lambda h, i: (row(i), 0)),
                pl.BlockSpec((lr, gk), lambda h, i: (0, h)),
                pl.BlockSpec((1, gk), lambda h, i: (0, h))]

    fwd_row = lambda i: i
    o_fwd = pl.pallas_call(
        _gla_fwd_kernel,
        grid=(ng, nt),
        in_specs=specs(fwd_row),
        out_specs=pl.BlockSpec((tc, gv), lambda h, i: (i, h)),
        out_shape=jax.ShapeDtypeStruct((s, GLA_HEADS * dv), F32),
        scratch_shapes=[pltpu.VMEM((grp, dv, dk), F32)],
        compiler_params=_cparams("parallel", "arbitrary"),
        name="gla_fwd",
    )(proj, proj, proj, a_lr, wa_f, ba_f)

    bwd_row = lambda i: nt - 1 - i
    return pl.pallas_call(
        _gla_bwd_kernel,
        grid=(ng, nt),
        in_specs=specs(bwd_row) + [
            pl.BlockSpec((tc, gv), lambda h, i: (bwd_row(i), h)),
            pl.BlockSpec((tc, gv), lambda h, i: (bwd_row(i), col_g // gv + h)),
            pl.BlockSpec((1, dv), lambda h, i: (0, 0))],
        out_specs=pl.BlockSpec((tc, gv), lambda h, i: (bwd_row(i), h)),
        out_shape=jax.ShapeDtypeStruct((s, GLA_HEADS * dv), BF16),
        scratch_shapes=[pltpu.VMEM((grp, dv, dk), F32)],
        compiler_params=_cparams("parallel", "arbitrary"),
        name="gla_bwd",
    )(proj, proj, proj, a_lr, wa_b, ba_b, o_fwd, proj, head_norm)


def _rope_tables_kernel(pos_ref, invf_ref, cos_ref, sina_ref, sinb_ref, *, axis):
    ang = pos_ref[...].astype(F32) * invf_ref[...]
    feat = lax.broadcasted_iota(I32, ang.shape, axis)
    half = MLA_ROPE // 2
    c, s_ = jnp.cos(ang), jnp.sin(ang)
    cos_ref[...] = jnp.where(feat < 2 * half, c, 0.0)
    sina_ref[...] = jnp.where(feat < half, -s_, 0.0)
    sinb_ref[...] = jnp.where((feat >= half) & (feat < 2 * half), s_, 0.0)


def rope_tables(positions, axis, tm=512):
    s = positions.shape[0]
    tm = _tile(s, tm)
    half = MLA_ROPE // 2
    inv_freq = ROPE_THETA ** (-jnp.arange(0, MLA_ROPE, 2, dtype=F32) / MLA_ROPE)
    invf = jnp.concatenate([inv_freq, inv_freq, jnp.zeros((LANES - 2 * half,), F32)])
    if axis == 1:
        pos, invf = positions.reshape(s, 1), invf.reshape(1, LANES)
        pos_spec, invf_spec = pl.BlockSpec((tm, 1), lambda i: (i, 0)), pl.BlockSpec((1, LANES), lambda i: (0, 0))
        spec, shape = pl.BlockSpec((tm, LANES), lambda i: (i, 0)), (s, LANES)
    else:
        pos, invf = positions.reshape(1, s), invf.reshape(LANES, 1)
        pos_spec, invf_spec = pl.BlockSpec((1, tm), lambda i: (0, i)), pl.BlockSpec((LANES, 1), lambda i: (0, 0))
        spec, shape = pl.BlockSpec((LANES, tm), lambda i: (0, i)), (LANES, s)
    return pl.pallas_call(
        functools.partial(_rope_tables_kernel, axis=axis),
        grid=(s // tm,),
        in_specs=[pos_spec, invf_spec],
        out_specs=[spec, spec, spec],
        out_shape=[jax.ShapeDtypeStruct(shape, F32)] * 3,
        compiler_params=_cparams("parallel"),
        name="rope_tables",
    )(pos, invf)


def _rope(x, cos_t, sin_a, sin_b, axis):
    half = MLA_ROPE // 2
    return x * cos_t + pltpu.roll(x, LANES - half, axis=axis) * sin_a + pltpu.roll(x, half, axis=axis) * sin_b


def _mla_post_kernel(d_ref, qn_ref, kvn_ref, cos_ref, sina_ref, sinb_ref, cq_ref, ckv_ref, kr_ref, *, q_lora, kv_lora):
    d = d_ref[...]
    cq_ref[...] = _rms(d[:, :q_lora], qn_ref[...]).astype(BF16)
    ckv_ref[...] = _rms(d[:, q_lora:q_lora + kv_lora], kvn_ref[...]).astype(BF16)
    kr = d[:, q_lora + kv_lora:]
    kr_ref[...] = _rope(kr, cos_ref[...], sina_ref[...], sinb_ref[...], 1).astype(BF16)


def mla_post(down, q_norm, kv_norm, tables, q_lora, kv_lora, tm=512):
    s, n = down.shape
    tm = _tile(s, tm)
    tab = pl.BlockSpec((tm, LANES), lambda i: (i, 0))
    return pl.pallas_call(
        functools.partial(_mla_post_kernel, q_lora=q_lora, kv_lora=kv_lora),
        grid=(s // tm,),
        in_specs=[pl.BlockSpec((tm, n), lambda i: (i, 0)),
                  pl.BlockSpec((1, q_lora), lambda i: (0, 0)),
                  pl.BlockSpec((1, kv_lora), lambda i: (0, 0)), tab, tab, tab],
        out_specs=[pl.BlockSpec((tm, q_lora), lambda i: (i, 0)),
                   pl.BlockSpec((tm, kv_lora), lambda i: (i, 0)),
                   pl.BlockSpec((tm, LANES), lambda i: (i, 0))],
        out_shape=[jax.ShapeDtypeStruct((s, q_lora), BF16),
                   jax.ShapeDtypeStruct((s, kv_lora), BF16),
                   jax.ShapeDtypeStruct((s, LANES), BF16)],
        compiler_params=_cparams("parallel"),
        name="mla_post",
    )(down, q_norm.reshape(1, -1), kv_norm.reshape(1, -1), *tables)


HEAD_GROUP = 4


def _q_up_kernel(x_ref, w_ref, cos_ref, sina_ref, sinb_ref, q_ref, *, qscale):
    x = x_ref[...]
    for g in range(w_ref.shape[0]):
        a = lax.dot_general(w_ref[g], x, NT_DIMS, preferred_element_type=F32)
        rot = _rope(a[MLA_NOPE:, :], cos_ref[...], sina_ref[...], sinb_ref[...], 0)
        q_ref[g] = (jnp.concatenate([a[:MLA_NOPE, :], rot], axis=0) * qscale).astype(BF16)


def mla_q_up(cq, w_uq_t, tables_t, qscale, tm=1024):
    s, r = cq.shape
    h_ = w_uq_t.shape[0]
    tm = _tile(s, tm)
    g = _tile(h_, HEAD_GROUP)
    tab = pl.BlockSpec((LANES, tm), lambda i, h: (0, i))
    return pl.pallas_call(
        functools.partial(_q_up_kernel, qscale=qscale),
        grid=(s // tm, h_ // g),
        in_specs=[pl.BlockSpec((tm, r), lambda i, h: (i, 0)),
                  pl.BlockSpec((g, MLA_QK_PAD, r), lambda i, h: (h, 0, 0)), tab, tab, tab],
        out_specs=pl.BlockSpec((g, MLA_QK_PAD, tm), lambda i, h: (h, 0, i)),
        out_shape=jax.ShapeDtypeStruct((h_, MLA_QK_PAD, s), BF16),
        compiler_params=_cparams("parallel", "arbitrary"),
        name="mla_q_up",
    )(cq, w_uq_t, *tables_t)


def _kv_up_kernel(x_ref, wk_ref, wvt_ref, kr_ref, k_ref, vt_ref):
    x = x_ref[...]
    for g in range(wk_ref.shape[0]):
        k_ref[g] = jnp.concatenate([_dot(x, wk_ref[g]).astype(BF16), kr_ref[...]], axis=1)
        vt = lax.dot_general(wvt_ref[g], x, NT_DIMS, preferred_element_type=F32).astype(BF16)
        ones = jnp.ones((vt_ref.shape[2] - MLA_V, vt.shape[1]), BF16)
        vt_ref[g, 0] = jnp.concatenate([vt, ones], axis=0)


V_ONES_ROWS = 16
ATTN_UNROLL = 8


ATTN_KEY_BLOCK = 512


def mla_kv_up(ckv, w_uk_h, w_uv_t, kr, tm=ATTN_KEY_BLOCK):
    s, r = ckv.shape
    h_ = w_uk_h.shape[0]
    tm = _tile(s, tm)
    g = _tile(h_, HEAD_GROUP)
    return pl.pallas_call(
        _kv_up_kernel,
        grid=(s // tm, h_ // g),
        in_specs=[pl.BlockSpec((tm, r), lambda i, h: (i, 0)),
                  pl.BlockSpec((g, r, MLA_NOPE), lambda i, h: (h, 0, 0)),
                  pl.BlockSpec((g, MLA_V, r), lambda i, h: (h, 0, 0)),
                  pl.BlockSpec((tm, LANES), lambda i, h: (i, 0))],
        out_specs=[pl.BlockSpec((g, tm, MLA_QK_PAD), lambda i, h: (h, i, 0)),
                   pl.BlockSpec((g, 1, MLA_V + V_ONES_ROWS, tm), lambda i, h: (h, i, 0, 0))],
        out_shape=[jax.ShapeDtypeStruct((h_, s, MLA_QK_PAD), BF16),
                   jax.ShapeDtypeStruct((h_, s // tm, MLA_V + V_ONES_ROWS, tm), BF16)],
        compiler_params=_cparams("parallel", "arbitrary"),
        name="mla_kv_up",
    )(ckv, w_uk_h, w_uv_t, kr)


def _attn_kernel(qt_ref, k_ref, vt_ref, o_ref, s_ref, acc_ref, *, tkb):
    nkb = vt_ref.shape[1]
    tq = qt_ref.shape[2]
    qt = qt_ref[0]

    def scores(j, slot):
        off = pl.multiple_of(j * tkb, tkb)
        s = _dot(k_ref[0, pl.ds(off, tkb), :], qt)
        s_ref[slot] = s
        return jnp.max(s, axis=0, keepdims=True)

    def softmax_pv(j, slot, m, blk_max):
        m_new = jnp.maximum(m, blk_max)
        alpha = jnp.exp2(m - m_new)
        p = jnp.exp2((s_ref[slot] - m_new).astype(BF16))
        acc_ref[...] = alpha * acc_ref[...] + _dot(vt_ref[0, j], p)
        return m_new

    acc_ref[...] = jnp.zeros_like(acc_ref)
    m = jnp.full((1, tq), -jnp.inf, F32)
    unroll = min(ATTN_UNROLL, nkb)

    def trip(base, carry, last):
        m, blk_max = carry
        for u in range(unroll):
            nxt_max = blk_max if (last and u == unroll - 1) else scores(base + u + 1, (u + 1) % 2)
            m = softmax_pv(base + u, u % 2, m, blk_max)
            blk_max = nxt_max
        return m, blk_max

    carry = (m, scores(0, 0))
    carry = lax.fori_loop(0, nkb // unroll - 1, lambda i, c: trip(i * unroll, c, False), carry)
    trip(nkb - unroll, carry, True)
    dv = o_ref.shape[1]
    o_ref[...] = (acc_ref[:dv, :] / acc_ref[dv:dv + 1, :]).T.astype(o_ref.dtype)


def mla_attention(qt, k, vt, tq=512):
    h_, dq, s = qt.shape
    _, nkb, dve, tkb = vt.shape
    dv = dve - V_ONES_ROWS
    tq = _tile(s, tq)
    assert nkb % min(ATTN_UNROLL, nkb) == 0 and (min(ATTN_UNROLL, nkb) % 2 == 0 or nkb == 1)
    return pl.pallas_call(
        functools.partial(_attn_kernel, tkb=tkb),
        grid=(h_, s // tq),
        in_specs=[pl.BlockSpec((1, dq, tq), lambda h, i: (h, 0, i)),
                  pl.BlockSpec((1, s, dq), lambda h, i: (h, 0, 0)),
                  pl.BlockSpec((1, nkb, dve, tkb), lambda h, i: (h, 0, 0, 0))],
        out_specs=pl.BlockSpec((tq, dv), lambda h, i: (i, h)),
        out_shape=jax.ShapeDtypeStruct((s, h_ * dv), BF16),
        scratch_shapes=[pltpu.VMEM((2, tkb, tq), F32), pltpu.VMEM((dve, tq), F32)],
        compiler_params=_cparams("parallel", "arbitrary"),
        name="mla_attention",
    )(qt, k, vt)


def _router_kernel(h_ref, g_ref, wt_ref, aff_ref):
    xn = _rms(h_ref[...], g_ref[...])
    x0, x1, _ = _split3(xn)
    w0, w1, _ = _split3(wt_ref[...])

    def nt(a, b):
        return lax.dot_general(a, b, NT_DIMS, preferred_element_type=F32)

    logits = (nt(w1, x0) + nt(w0, x1)) + nt(w0, x0)
    e = jnp.exp(logits - jnp.max(logits, axis=0, keepdims=True))
    aff_ref[...] = e / jnp.sum(e, axis=0, keepdims=True)


def router_affinity(h, gain, w_router_t, tm=512):
    s, d = h.shape
    e_ = w_router_t.shape[0]
    tm = _tile(s, tm)
    return pl.pallas_call(
        _router_kernel,
        grid=(s // tm,),
        in_specs=[pl.BlockSpec((tm, d), lambda i: (i, 0)),
                  pl.BlockSpec((1, d), lambda i: (0, 0)),
                  pl.BlockSpec((e_, d), lambda i: (0, 0))],
        out_specs=pl.BlockSpec((e_, tm), lambda i: (0, i)),
        out_shape=jax.ShapeDtypeStruct((e_, s), F32),
        compiler_params=_cparams("parallel"),
        name="router_affinity",
    )(h, gain.reshape(1, d), w_router_t)


def _prefix_parts(xb, u_incl, ones_b, sl_strict):
    loc = _dot(xb, u_incl)
    tot = _dot(xb, ones_b)
    base = _dot(sl_strict, tot.astype(BF16))
    return loc, tot, base


def _select_kernel(aff_ref, idx_ref, gate_ref, rank_ref, start_ref, end_ref, sel_ref, *, cap):
    e_, nt, ln = aff_ref.shape
    li = lax.broadcasted_iota(I32, (ln, ln), 0)
    lj = lax.broadcasted_iota(I32, (ln, ln), 1)
    u_incl = jnp.where(li <= lj, 1.0, 0.0).astype(BF16)
    u_incl_t = jnp.where(lj <= li, 1.0, 0.0).astype(BF16)
    ones_b = jnp.ones((ln, ln), BF16)
    ti = lax.broadcasted_iota(I32, (nt, nt), 0)
    tj = lax.broadcasted_iota(I32, (nt, nt), 1)
    sl_strict = jnp.where(tj < ti, 1.0, 0.0).astype(BF16)

    bits = pltpu.bitcast(aff_ref[...], I32)

    def count_ge(thr):
        ge = jnp.where(bits >= thr, 1.0, 0.0)
        return jnp.sum(jnp.sum(ge, axis=2, keepdims=True), axis=1, keepdims=True)

    def search(b, thr):
        cand = thr | jnp.left_shift(jnp.int32(1), 30 - b)
        return jnp.where(count_ge(cand) >= cap, cand, thr)

    thr = lax.fori_loop(0, 31, search, jnp.zeros((e_, 1, 1), I32))
    gt = jnp.where(bits > thr, 1.0, 0.0)
    eq = jnp.where(bits == thr, 1.0, 0.0)
    need = cap - jnp.sum(jnp.sum(gt, axis=2, keepdims=True), axis=1, keepdims=True)

    cnt = jnp.zeros((nt, ln), F32)
    for e in range(e_):
        loc, _, base = _prefix_parts(eq[e].astype(BF16), u_incl, ones_b, sl_strict)
        eq_rank = loc + base - eq[e]
        sel = gt[e] + eq[e] * jnp.where(eq_rank < need[e], 1.0, 0.0)
        sel_ref[e] = sel
        cnt = cnt + sel

    cnt_b = cnt.astype(BF16)
    loc_c = _dot(cnt_b, u_incl)
    tot_c = _dot(cnt_b, ones_b)
    t0, t1, t2 = _split3(tot_c)
    base_c = (_dot(sl_strict, t0) + _dot(sl_strict, t1)) + _dot(sl_strict, t2)
    end = loc_c + base_c
    start = end - cnt
    start_ref[...] = start.astype(I32)
    end_ref[...] = end.astype(I32)

    p_row = lax.broadcasted_iota(I32, (1, cap), 1).astype(F32)
    tile_iota = lax.broadcasted_iota(I32, (nt, cap), 0).astype(F32)
    lane_iota = lax.broadcasted_iota(I32, (ln, cap), 0).astype(F32)

    def compact(e, within):
        sel = sel_ref[e]
        selb = sel.astype(BF16)
        _, tot, base = _prefix_parts(selb, u_incl, ones_b, sl_strict)
        tile_end = (base + tot)[:, 0:1]
        jp = jnp.sum(jnp.where(tile_end <= p_row, 1.0, 0.0), axis=0, keepdims=True)
        g_t = jnp.where(tile_iota == jp, 1.0, 0.0)
        base_p = jnp.sum(g_t * base[:, 0:1], axis=0, keepdims=True)
        r_p = p_row - base_p
        loc_t = lax.dot_general(u_incl_t, selb, NT_DIMS, preferred_element_type=F32)
        sel_loc = _dot(loc_t.astype(BF16), g_t.astype(BF16))
        lane_p = jnp.sum(jnp.where(sel_loc <= r_p, 1.0, 0.0), axis=0, keepdims=True)
        l_t = jnp.where(lane_iota == lane_p, 1.0, 0.0)
        l_b = l_t.astype(BF16)
        idx_ref[e] = (jp * ln + lane_p).astype(I32)
        gate_ref[e] = jnp.sum(g_t * _dot_exact_lhs(aff_ref[e], l_b), axis=0, keepdims=True)
        rank_full = start + within
        rank_ref[e] = jnp.sum(g_t * _dot_exact_lhs(rank_full, l_b), axis=0, keepdims=True).astype(I32)
        return within + sel

    lax.fori_loop(0, e_, compact, jnp.zeros((nt, ln), F32))


def expert_select(aff3, cap):
    e_, nt, ln = aff3.shape
    full3 = lambda a, b, c: pl.BlockSpec((a, b, c), lambda i: (0, 0, 0))
    full2 = lambda a, b: pl.BlockSpec((a, b), lambda i: (0, 0))
    return pl.pallas_call(
        functools.partial(_select_kernel, cap=cap),
        grid=(1,),
        in_specs=[full3(e_, nt, ln)],
        out_specs=[full3(e_, 1, cap), full3(e_, 1, cap), full3(e_, 1, cap), full2(nt, ln), full2(nt, ln)],
        out_shape=[jax.ShapeDtypeStruct((e_, 1, cap), I32),
                   jax.ShapeDtypeStruct((e_, 1, cap), F32),
                   jax.ShapeDtypeStruct((e_, 1, cap), I32),
                   jax.ShapeDtypeStruct((nt, ln), I32),
                   jax.ShapeDtypeStruct((nt, ln), I32)],
        scratch_shapes=[pltpu.VMEM((e_, nt, ln), F32)],
        compiler_params=_cparams("arbitrary"),
        name="expert_select",
    )(aff3)


DMA_ISSUE_UNROLL = 8


def _row_gather_copy(h_hbm, xbuf, sem, slot, tok, p):
    return pltpu.make_async_copy(h_hbm.at[pl.ds(tok, 1)], xbuf.at[slot, pl.ds(p, 1)], sem.at[slot])


def _ffn_up_kernel(idx_ref, h_hbm, g_ref, wg_ref, wu_ref, hid_ref, xbuf, sem):
    tm = xbuf.shape[1]
    step = pl.program_id(0) * pl.num_programs(1) + pl.program_id(1)
    nsteps = pl.num_programs(0) * pl.num_programs(1)
    slot = step % 2

    def issue(s_, slot_):
        def body(p, carry):
            _row_gather_copy(h_hbm, xbuf, sem, slot_, idx_ref[s_ * tm + p], p).start()
            return carry
        lax.fori_loop(0, tm, body, 0, unroll=DMA_ISSUE_UNROLL)

    @pl.when(step == 0)
    def _():
        issue(step, slot)

    @pl.when(step + 1 < nsteps)
    def _():
        issue(step + 1, 1 - slot)

    pltpu.make_async_copy(h_hbm.at[pl.ds(0, tm)], xbuf.at[slot], sem.at[slot]).wait()
    xn = _rms(xbuf[slot], g_ref[...]).astype(BF16)
    gate = _dot(xn, wg_ref[0])
    up = _dot(xn, wu_ref[0])
    hid_ref[...] = (gate * jax.nn.sigmoid(gate) * up).astype(hid_ref.dtype)


def ffn_up(h, gain, idx_flat, w_gate, w_up, w_base, e_, cap, tm=512):
    s, d = h.shape
    _, _, ff = w_gate.shape
    tm = _tile(cap, tm)
    r = cap // tm
    grid_spec = pltpu.PrefetchScalarGridSpec(
        num_scalar_prefetch=1,
        grid=(e_, r),
        in_specs=[pl.BlockSpec(memory_space=pl.ANY),
                  pl.BlockSpec((1, d), lambda e, i, idx: (0, 0)),
                  pl.BlockSpec((1, d, ff), lambda e, i, idx: (w_base + e, 0, 0)),
                  pl.BlockSpec((1, d, ff), lambda e, i, idx: (w_base + e, 0, 0))],
        out_specs=pl.BlockSpec((tm, ff), lambda e, i, idx: (e * r + i, 0)),
        scratch_shapes=[pltpu.VMEM((2, tm, d), F32), pltpu.SemaphoreType.DMA((2,))],
    )
    return pl.pallas_call(
        _ffn_up_kernel,
        grid_spec=grid_spec,
        out_shape=jax.ShapeDtypeStruct((e_ * cap, ff), BF16),
        compiler_params=_cparams("arbitrary", "arbitrary"),
        name="ffn_up",
    )(idx_flat, h, gain.reshape(1, d), w_gate, w_up)


def _pack_bf16_pairs(lo, hi):
    lo_bits = lax.bitcast_convert_type(lo.astype(BF16).astype(F32), jnp.uint32)
    hi_bits = lax.bitcast_convert_type(hi.astype(BF16).astype(F32), jnp.uint32)
    return (hi_bits & jnp.uint32(0xFFFF0000)) | (lo_bits >> 16)


def _unpack_bf16_pairs(w):
    lo = lax.bitcast_convert_type(w << 16, F32).astype(BF16)
    hi = lax.bitcast_convert_type(w & jnp.uint32(0xFFFF0000), F32).astype(BF16)
    return lo, hi


def _row_scatter_copy(ybuf, y_hbm, sem, slot, p, row):
    return pltpu.make_async_copy(ybuf.at[slot, pl.ds(p, 1)], y_hbm.at[pl.ds(row, 1)], sem.at[slot])


def _ffn_down_kernel(rank_ref, hid_ref, wd_ref, gate_ref, y_hbm, ybuf, sem):
    tm, half = ybuf.shape[1], ybuf.shape[2]
    step = pl.program_id(0) * pl.num_programs(1) + pl.program_id(1)
    nsteps = pl.num_programs(0) * pl.num_programs(1)
    slot = step % 2

    def drain(slot_):
        pltpu.make_async_copy(ybuf.at[slot_], y_hbm.at[pl.ds(0, tm)], sem.at[slot_]).wait()

    @pl.when(step >= 2)
    def _():
        drain(slot)

    y = _dot(hid_ref[...], wd_ref[0]) * gate_ref[...]
    ybuf[slot] = _pack_bf16_pairs(y[:, :half], y[:, half:])

    def body(p, carry):
        _row_scatter_copy(ybuf, y_hbm, sem, slot, p, rank_ref[step * tm + p]).start()
        return carry
    lax.fori_loop(0, tm, body, 0, unroll=DMA_ISSUE_UNROLL)

    @pl.when(step == nsteps - 1)
    def _():
        drain(slot)

        @pl.when(nsteps >= 2)
        def _():
            drain(1 - slot)


def ffn_down(hid, w_down, w_base, e_, gate_col, rank_flat, cap, tm=512):
    n, ff = hid.shape
    _, _, d = w_down.shape
    tm = _tile(cap, tm)
    r = cap // tm
    grid_spec = pltpu.PrefetchScalarGridSpec(
        num_scalar_prefetch=1,
        grid=(e_, r),
        in_specs=[pl.BlockSpec((tm, ff), lambda e, i, rk: (e * r + i, 0)),
                  pl.BlockSpec((1, ff, d), lambda e, i, rk: (w_base + e, 0, 0)),
                  pl.BlockSpec((tm, 1), lambda e, i, rk: (e * r + i, 0))],
        out_specs=pl.BlockSpec(memory_space=pl.ANY),
        scratch_shapes=[pltpu.VMEM((2, tm, d // 2), jnp.uint32), pltpu.SemaphoreType.DMA((2,))],
    )
    return pl.pallas_call(
        _ffn_down_kernel,
        grid_spec=grid_spec,
        out_shape=jax.ShapeDtypeStruct((n, d // 2), jnp.uint32),
        compiler_params=_cparams("arbitrary", "arbitrary"),
        name="ffn_down",
    )(rank_flat, hid, w_down, gate_col)


def _combine_kernel(ws_ref, vb_ref, h_ref, st_ref, en_ref, y_hbm, *rest, nchunks, out_norm):
    if out_norm:
        gain_ref, o_ref, win, sem = rest
    else:
        o_ref, win, sem = rest
    i, nt = pl.program_id(0), pl.num_programs(0)
    wrows, half = win.shape[1], win.shape[2]

    def first_chunk(t):
        return jnp.minimum(ws_ref[t] // wrows, nchunks - 1)

    def chunk_copy(chunk, slot):
        row0 = pl.multiple_of(chunk * wrows, wrows)
        return pltpu.make_async_copy(y_hbm.at[pl.ds(row0, wrows)], win.at[slot], sem.at[slot])

    lo_c = first_chunk(i)
    v0 = vb_ref[i]
    n_i = vb_ref[i + 1] - v0

    @pl.when(i == 0)
    def _():
        chunk_copy(lo_c, 0).start()

    st, en = st_ref[...], en_ref[...]
    o_ref[...] = h_ref[...]

    def visit(k, carry):
        slot = (v0 + k) % 2
        chunk = lo_c + k
        last_in_tile = k == n_i - 1
        nxt = jnp.where(last_in_tile, first_chunk(jnp.minimum(i + 1, nt - 1)), chunk + 1)

        @pl.when(jnp.logical_or(jnp.logical_not(last_in_tile), i + 1 < nt))
        def _():
            chunk_copy(nxt, 1 - slot).start()

        chunk_copy(chunk, slot).wait()
        r_abs = chunk * wrows + lax.broadcasted_iota(I32, (1, wrows), 1)
        q = jnp.where((st <= r_abs) & (r_abs < en), 1.0, 0.0).astype(BF16)
        lo, hi = _unpack_bf16_pairs(win[slot])
        o_ref[:, :half] += _dot(q, lo)
        o_ref[:, half:] += _dot(q, hi)
        return carry

    lax.fori_loop(0, n_i, visit, 0)
    if out_norm:
        o_ref[...] = _rms(o_ref[...], gain_ref[...])


def _combine_visits(win_starts, wrows, nchunks):
    lo_c = jnp.minimum(win_starts[:-1] // wrows, nchunks - 1)
    hi_c = jnp.maximum(lo_c, (win_starts[1:] + wrows - 1) // wrows - 1)
    return hi_c - lo_c + 1


def ffn_combine(h, y_sorted, start_col, end_col, win_starts, tm, out_gain=None, wrows=256):
    s, d = h.shape
    total = y_sorted.shape[0]
    wrows = _tile(total, wrows)
    nchunks = total // wrows
    n_visits = _combine_visits(win_starts, wrows, nchunks)
    visit_base = jnp.concatenate([jnp.zeros((1,), I32), jnp.cumsum(n_visits, dtype=I32)])
    in_specs = [pl.BlockSpec((tm, d), lambda i, ws, vb: (i, 0)),
                pl.BlockSpec((tm, 1), lambda i, ws, vb: (i, 0)),
                pl.BlockSpec((tm, 1), lambda i, ws, vb: (i, 0)),
                pl.BlockSpec(memory_space=pl.ANY)]
    operands = [win_starts, visit_base, h, start_col, end_col, y_sorted]
    if out_gain is not None:
        in_specs.append(pl.BlockSpec((1, d), lambda i, ws, vb: (0, 0)))
        operands.append(out_gain.reshape(1, d))
    grid_spec = pltpu.PrefetchScalarGridSpec(
        num_scalar_prefetch=2,
        grid=(s // tm,),
        in_specs=in_specs,
        out_specs=pl.BlockSpec((tm, d), lambda i, ws, vb: (i, 0)),
        scratch_shapes=[pltpu.VMEM((2, wrows, d // 2), jnp.uint32), pltpu.SemaphoreType.DMA((2,))],
    )
    return pl.pallas_call(
        functools.partial(_combine_kernel, nchunks=nchunks, out_norm=out_gain is not None),
        grid_spec=grid_spec,
        out_shape=jax.ShapeDtypeStruct((s, d), F32),
        compiler_params=_cparams("arbitrary"),
        name="ffn_combine",
    )(*operands)


def expert_choice_ffn(h, gain, w_router_t, w_gate, w_up, w_down, w_base, out_gain=None, combine_tm=256):
    s, d = h.shape
    e_ = w_router_t.shape[0]
    cap = CAPACITY_FACTOR * s // e_
    aff = router_affinity(h, gain, w_router_t)
    idx, gate, rank, start, end = expert_select(aff.reshape(e_, s // LANES, LANES), cap)
    hid = ffn_up(h, gain, idx.reshape(-1), w_gate, w_up, w_base, e_, cap)
    y_sorted = ffn_down(hid, w_down, w_base, e_, gate.reshape(-1, 1), rank.reshape(-1), cap)
    start_flat = start.reshape(-1)
    tm = _tile(s, combine_tm)
    win_starts = jnp.concatenate([start_flat[::tm], jnp.full((1,), e_ * cap, I32)])
    return ffn_combine(h, y_sorted, start_flat.reshape(s, 1), end.reshape(s, 1), win_starts, tm, out_gain)


def conv_gla_layer(h, norm, w_in, conv_w, wa2_f, ba_f, wa2_b, ba_b, head_norm, w_out):
    d = h.shape[1]
    width = conv_w.shape[1]
    kdim = wa2_f.shape[1]
    dk = kdim // GLA_HEADS
    dv = head_norm.shape[0]
    vdim = dv * GLA_HEADS
    n_main = 3 * width + 2 * kdim + 2 * vdim
    lr = 2 * GLA_GATE_RANK
    col_q = 3 * width
    col_k, col_v, col_g = col_q + kdim, col_q + 2 * kdim, col_q + 2 * kdim + vdim

    w_lr = jnp.pad(w_in[:, n_main:], ((0, 0), (0, LANES - lr))).astype(BF16)
    proj = norm_matmul(h, norm, w_in.astype(BF16), BF16, tn=1024, n=n_main)
    a_lr = norm_matmul(h, norm, w_lr, F32, tn=LANES)
    wa_f = jnp.pad(wa2_f, ((0, LANES - GLA_GATE_RANK), (0, 0)))
    wa_b = jnp.pad(wa2_b, ((GLA_GATE_RANK, LANES - lr), (0, 0)))
    y_conv = gated_conv(proj, conv_w, width)
    y_gla = gla_mixer(proj, a_lr, wa_f, ba_f.reshape(1, -1), wa_b, ba_b.reshape(1, -1),
                      head_norm.reshape(1, -1), col_q, col_k, col_v, col_g, dk, dv)
    w_out_b = w_out.astype(BF16)
    return matmul_residual([y_conv, y_gla], [w_out_b[:width], w_out_b[width:]], h)


def mla_layer(h, tables, norm, w_down, q_norm, kv_norm, w_uq, w_ukv, w_out):
    q_lora, kv_lora = q_norm.shape[0], kv_norm.shape[0]
    heads = w_uq.shape[1] // (MLA_NOPE + MLA_ROPE)
    n_down = w_down.shape[1]
    n_pad = q_lora + kv_lora + LANES
    w_down_b = jnp.pad(w_down, ((0, 0), (0, n_pad - n_down))).astype(BF16)
    down = norm_matmul(h, norm, w_down_b, F32, tm=256, tn=n_pad)
    tables, tables_t = tables
    cq, ckv, kr = mla_post(down, q_norm, kv_norm, tables, q_lora, kv_lora)
    w_uq_t = w_uq.reshape(q_lora, heads, MLA_NOPE + MLA_ROPE).transpose(1, 2, 0)
    w_uq_t = jnp.pad(w_uq_t, ((0, 0), (0, MLA_QK_PAD - MLA_NOPE - MLA_ROPE), (0, 0))).astype(BF16)
    w_ukv_h = w_ukv.reshape(kv_lora, heads, MLA_NOPE + MLA_V)
    w_uk_h = w_ukv_h[:, :, :MLA_NOPE].transpose(1, 0, 2).astype(BF16)
    w_uv_t = w_ukv_h[:, :, MLA_NOPE:].transpose(1, 2, 0).astype(BF16)
    qscale = (MLA_NOPE + MLA_ROPE) ** -0.5 * math.log2(math.e)
    qt = mla_q_up(cq, w_uq_t, tables_t, qscale)
    k, vt = mla_kv_up(ckv, w_uk_h, w_uv_t, kr)
    o = mla_attention(qt, k, vt)
    return matmul_residual([o], [w_out.astype(BF16)], h)


def kernel(x, positions, ab_norm, ab_w_in, ab_conv_w, gla_wa2_fwd, gla_ba_fwd, gla_wa2_bwd, gla_ba_bwd, gla_head_norm, ab_w_out, mla_norm, mla_w_down, mla_q_norm, mla_kv_norm, mla_w_uq, mla_w_ukv, mla_w_out, ffn_norm, router_w, expert_w_gate, expert_w_up, expert_w_down, final_norm):
    b_, s, d = x.shape
    depth = ffn_norm.shape[0]
    n_exp = expert_w_gate.shape[1]
    w_gate_all = expert_w_gate.astype(BF16).reshape((depth * n_exp,) + expert_w_gate.shape[2:])
    w_up_all = expert_w_up.astype(BF16).reshape((depth * n_exp,) + expert_w_up.shape[2:])
    w_down_all = expert_w_down.astype(BF16).reshape((depth * n_exp,) + expert_w_down.shape[2:])
    outs = []
    for b in range(b_):
        h = x[b]
        tables = (rope_tables(positions[b], 1), rope_tables(positions[b], 0))
        for i in range(depth):
            j = i // 2
            if i % 2 == 0:
                h = conv_gla_layer(h, ab_norm[j], ab_w_in[j], ab_conv_w[j], gla_wa2_fwd[j], gla_ba_fwd[j],
                                   gla_wa2_bwd[j], gla_ba_bwd[j], gla_head_norm[j], ab_w_out[j])
            else:
                h = mla_layer(h, tables, mla_norm[j], mla_w_down[j], mla_q_norm[j], mla_kv_norm[j],
                              mla_w_uq[j], mla_w_ukv[j], mla_w_out[j])
            h = expert_choice_ffn(h, ffn_norm[i], router_w[i].T, w_gate_all, w_up_all, w_down_all, i * n_exp,
                                  out_gain=final_norm if i == depth - 1 else None)
        outs.append(h)
    return jnp.stack(outs)
```
